```python
import jax, jax.numpy as jnp
from jax import lax
import numpy as np

D_MODEL = 1024
BATCH = 2
SEQ = 16384
DEPTH = 2

GRID_W = 64
CTX_LEN = 256
N_MIXERS = 2
N_GLA_LAYERS = (DEPTH + 1) // 2
N_CONV_LAYERS = DEPTH // 2

GLA_HEADS = 4
GLA_DK = D_MODEL // 2
GLA_DV = D_MODEL
GLA_HK = GLA_DK // GLA_HEADS
GLA_HV = GLA_DV // GLA_HEADS
GLA_RANK = 16
GLA_TAU = 16.0
GLA_CHUNK = 64

CONV_WIDTH = 3

N_EXPERTS = 16
EC_CAPACITY_FACTOR = 2
D_EXPERT = 2 * D_MODEL

EPS = 1e-6

kernel_name = "hybrid_gla_shortconv_ecmoe_diffusion_prefix"


def rmsnorm(x, g):
    xf = x.astype(jnp.float32)
    y = xf * lax.rsqrt(jnp.mean(xf * xf, axis=-1, keepdims=True) + EPS)
    return (y * g.astype(jnp.float32)).astype(x.dtype)


def modulate(h, shift, scale):
    return h * (1 + scale) + shift


def _to_chunks(t):
    b, t_len, h, d = t.shape
    return t.reshape(b, t_len // GLA_CHUNK, GLA_CHUNK, h, d).transpose(1, 0, 3, 2, 4)


def gla_chunk_scan(q, k, v, log_a, s0):
    b, t_len, h, _ = q.shape
    qc, kc, vc, gc = _to_chunks(q), _to_chunks(k), _to_chunks(v), _to_chunks(log_a)
    tri = jnp.tril(jnp.ones((GLA_CHUNK, GLA_CHUNK), dtype=bool))[:, :, None]

    def step(state, inp):
        qi, ki, vi, gi = inp
        a_cum = jnp.cumsum(gi.astype(jnp.float32), axis=2)
        a_last = a_cum[:, :, -1:, :]
        diff = a_cum[:, :, :, None, :] - a_cum[:, :, None, :, :]
        decay = jnp.exp(jnp.where(tri, diff, -jnp.inf))
        scores = jnp.einsum('bhid,bhjd,bhijd->bhij', qi, ki, decay)
        o_intra = jnp.einsum('bhij,bhjv->bhiv', scores, vi)
        o_inter = jnp.einsum('bhid,bhdv->bhiv', qi * jnp.exp(a_cum), state)
        k_dec = ki * jnp.exp(a_last - a_cum)
        new_state = state * jnp.exp(a_last)[:, :, 0, :, None] + jnp.einsum('bhjd,bhjv->bhdv', k_dec, vi)
        return new_state, (o_intra + o_inter)

    s_final, o = lax.scan(step, s0, (qc, kc, vc, gc))
    o = o.transpose(1, 0, 3, 2, 4).reshape(b, t_len, h, GLA_HV)
    return o.astype(v.dtype), s_final


def gla_bidirectional(q, k, v, la_f, la_b, s0_f, s0_b):
    o_f, s_f = gla_chunk_scan(q, k, v, la_f, s0_f)
    flip = lambda t: jnp.flip(t, axis=1)
    o_b, s_b = gla_chunk_scan(flip(q), flip(k), flip(v), flip(la_b), s0_b)
    return o_f + flip(o_b), s_f, s_b


def gla_project(h, w_in, w_a2, b_a2):
    b, t_len, _ = h.shape
    sizes = [GLA_DK, GLA_DK, GLA_DV, GLA_DV, GLA_RANK, GLA_RANK]
    q, k, v, r, a_f, a_b = jnp.split(h @ w_in, list(np.cumsum(sizes)[:-1]), axis=-1)
    q = q.reshape(b, t_len, GLA_HEADS, GLA_HK) * (GLA_HK ** -0.5)
    k = k.reshape(b, t_len, GLA_HEADS, GLA_HK)
    v = v.reshape(b, t_len, GLA_HEADS, GLA_HV)

    def log_gate(a, d):
        z = (a @ w_a2[d] + b_a2[d]).astype(jnp.float32)
        return (jax.nn.log_sigmoid(z) / GLA_TAU).reshape(b, t_len, GLA_HEADS, GLA_HK)

    return q, k, v, r, log_gate(a_f, 0), log_gate(a_b, 1)


def gla_finish(o, r, g_norm, w_out):
    b, t_len = o.shape[:2]
    o = rmsnorm(o, g_norm) * jax.nn.silu(r).reshape(b, t_len, GLA_HEADS, GLA_HV)
    return o.reshape(b, t_len, GLA_DV) @ w_out


def gla_mixer(h_x, h_c, w_in, w_a2, b_a2, g_norm, w_out, need_ctx_out):
    b = h_x.shape[0]
    qc, kc, vc, rc, lfc, lbc = gla_project(h_c, w_in, w_a2, b_a2)
    s0 = jnp.zeros((b, GLA_HEADS, GLA_HK, GLA_HV), jnp.float32)
    o_c, s_cf, s_cb = gla_bidirectional(qc, kc, vc, lfc, lbc, s0, s0)
    qx, kx, vx, rx, lfx, lbx = gla_project(h_x, w_in, w_a2, b_a2)
    o_x, _, _ = gla_bidirectional(qx, kx, vx, lfx, lbx, s_cf, s_cb)
    y_x = gla_finish(o_x, rx, g_norm, w_out)
    y_c = gla_finish(o_c, rc, g_norm, w_out) if need_ctx_out else None
    return y_x, y_c


def shortconv_mixer(h, w_in, conv_k, w_out, n_seq, seg_len):
    bg, cg, v = jnp.split(h @ w_in, 3, axis=-1)
    u = (cg * v).reshape(n_seq, seg_len, D_MODEL)
    u = lax.conv_general_dilated(
        u, conv_k.reshape(CONV_WIDTH, 1, D_MODEL).astype(u.dtype),
        window_strides=(1,), padding=[((CONV_WIDTH - 1) // 2, (CONV_WIDTH - 1) // 2)],
        dimension_numbers=('NWC', 'WIO', 'NWC'), feature_group_count=D_MODEL)
    return (bg * u.reshape(h.shape)) @ w_out


def ec_moe(h, w_router, w_gate, w_up, w_down):
    b, t_len, d = h.shape
    cap = EC_CAPACITY_FACTOR * t_len // N_EXPERTS
    aff = jax.nn.softmax((h @ w_router).astype(jnp.float32), axis=-1)
    vals, idx = lax.top_k(jnp.swapaxes(aff, 1, 2), cap)
    idx_flat = idx.reshape(b, N_EXPERTS * cap)
    xe = jnp.take_along_axis(h, idx_flat[..., None], axis=1).reshape(b, N_EXPERTS, cap, d)
    hid = jax.nn.silu(jnp.einsum('becd,edf->becf', xe, w_gate)) * jnp.einsum('becd,edf->becf', xe, w_up)
    ye = jnp.einsum('becf,efd->becd', hid, w_down) * vals[..., None].astype(h.dtype)
    scatter = lambda y, i: jnp.zeros((t_len, d), y.dtype).at[i].add(y)
    return jax.vmap(scatter)(ye.reshape(b, N_EXPERTS * cap, d), idx_flat)


def setup_inputs(seed: int = 0) -> dict:
    key = jax.random.key(seed)
    ks = jax.random.split(key, 24)
    nrm = lambda k, shape, scale: jax.random.normal(k, shape, jnp.float32) * scale
    d = D_MODEL
    gla_in_width = 2 * GLA_DK + 2 * GLA_DV + 2 * GLA_RANK
    return {
        "x": nrm(ks[0], (BATCH, SEQ, d), 1.0),
        "c": nrm(ks[1], (BATCH, d), 1.0),
        "ctx": nrm(ks[2], (BATCH, CTX_LEN, d), 1.0),
        "c_ctx": nrm(ks[3], (d,), 1.0),
        "ada_w": nrm(ks[4], (DEPTH, d, 6 * d), 0.5 * d ** -0.5),
        "ada_b": nrm(ks[5], (DEPTH, 6 * d), 0.01),
        "norm_g": 1.0 + nrm(ks[6], (DEPTH, 2, d), 0.05),
        "gla_w_in": nrm(ks[7], (N_GLA_LAYERS, d, gla_in_width), d ** -0.5),
        "gla_w_a2": nrm(ks[8], (N_GLA_LAYERS, 2, GLA_RANK, GLA_DK), GLA_RANK ** -0.5),
        "gla_b_a2": nrm(ks[9], (N_GLA_LAYERS, 2, GLA_DK), 0.1),
        "gla_norm_g": 1.0 + nrm(ks[10], (N_GLA_LAYERS, GLA_HV), 0.05),
        "gla_w_out": nrm(ks[11], (N_GLA_LAYERS, GLA_DV, d), GLA_DV ** -0.5),
        "conv_w_in": nrm(ks[12], (N_CONV_LAYERS, d, 3 * d), d ** -0.5),
        "conv_k": nrm(ks[13], (N_CONV_LAYERS, CONV_WIDTH, d), CONV_WIDTH ** -0.5),
        "conv_w_out": nrm(ks[14], (N_CONV_LAYERS, d, d), d ** -0.5),
        "router_w": nrm(ks[15], (DEPTH, d, N_EXPERTS), d ** -0.5),
        "expert_w_gate": nrm(ks[16], (DEPTH, N_EXPERTS, d, D_EXPERT), d ** -0.5),
        "expert_w_up": nrm(ks[17], (DEPTH, N_EXPERTS, d, D_EXPERT), d ** -0.5),
        "expert_w_down": nrm(ks[18], (DEPTH, N_EXPERTS, D_EXPERT, d), D_EXPERT ** -0.5),
        "final_norm_g": 1.0 + nrm(ks[19], (d,), 0.05),
    }


def reference(x, c, ctx, c_ctx, ada_w, ada_b, norm_g, gla_w_in, gla_w_a2, gla_b_a2, gla_norm_g,
              gla_w_out, conv_w_in, conv_k, conv_w_out, router_w, expert_w_gate, expert_w_up,
              expert_w_down, final_norm_g):
    b, t_len, d = x.shape
    rows = t_len // GRID_W
    ctx_len = ctx.shape[1]
    for i in range(DEPTH):
        last = i == DEPTH - 1
        j = i // N_MIXERS
        is_gla = (i % N_MIXERS) == 0
        ctx_needed = is_gla or not last
        mod_x = jax.nn.silu(c) @ ada_w[i] + ada_b[i]
        mod_c = jax.nn.silu(c_ctx) @ ada_w[i] + ada_b[i]
        sh1x, sc1x, g1x, sh2x, sc2x, g2x = jnp.split(mod_x[:, None, :], 6, axis=-1)
        sh1c, sc1c, g1c, sh2c, sc2c, g2c = jnp.split(mod_c, 6, axis=-1)

        h_x = modulate(rmsnorm(x, norm_g[i, 0]), sh1x, sc1x)
        h_c = modulate(rmsnorm(ctx, norm_g[i, 0]), sh1c, sc1c) if ctx_needed else None
        if is_gla:
            y_x, y_c = gla_mixer(h_x, h_c, gla_w_in[j], gla_w_a2[j], gla_b_a2[j], gla_norm_g[j],
                                 gla_w_out[j], need_ctx_out=not last)
        else:
            y_x = shortconv_mixer(h_x, conv_w_in[j], conv_k[j], conv_w_out[j], b * rows, GRID_W)
            y_c = (shortconv_mixer(h_c, conv_w_in[j], conv_k[j], conv_w_out[j], b, ctx_len)
                   if not last else None)
        x = x + g1x * y_x

        h2x = modulate(rmsnorm(x, norm_g[i, 1]), sh2x, sc2x)
        x = x + g2x * ec_moe(h2x, router_w[i], expert_w_gate[i], expert_w_up[i], expert_w_down[i])

        if not last:
            ctx = ctx + g1c * y_c
            h2c = modulate(rmsnorm(ctx, norm_g[i, 1]), sh2c, sc2c)
            ctx = ctx + g2c * ec_moe(h2c, router_w[i], expert_w_gate[i], expert_w_up[i], expert_w_down[i])
    return rmsnorm(x, final_norm_g)
```

```python
import functools

import jax
import jax.numpy as jnp
from jax import lax
from jax.experimental import pallas as pl
from jax.experimental.pallas import tpu as pltpu

F32 = jnp.float32
BF16 = jnp.bfloat16
I32 = jnp.int32
HIGHEST = lax.Precision.HIGHEST

EPS = 1e-6
GRID_W = 64
GLA_HEADS = 4
GLA_RANK = 16
GLA_TAU = 16.0
N_EXPERTS = 16
EC_CAPACITY_FACTOR = 2
CONV_WIDTH = 3

LANES = 128
ROW_ALIGN = 16
GLA_CHUNK = 256
GLA_DIAG = 8
ROW_CHUNK = 256
PIECES = ROW_CHUNK // ROW_ALIGN
MIN_NORMAL_BITS = 0x00800000
REFINE_STEPS = 40
FFN_F_TILE = 512
VMEM_LIMIT = 60 * 1024 * 1024


def _cparams(*sem):
    return pltpu.CompilerParams(dimension_semantics=sem, vmem_limit_bytes=VMEM_LIMIT)


def _dot(a, b):
    return jnp.dot(a, b, preferred_element_type=F32)


def _dot_nt(a, b):
    return lax.dot_general(a, b, (((1,), (1,)), ((), ())), preferred_element_type=F32)


def _dot_tn(a, b):
    return lax.dot_general(a, b, (((0,), (0,)), ((), ())), preferred_element_type=F32)


def _xdot(a, b):
    return jnp.dot(a, b, precision=HIGHEST, preferred_element_type=F32)


def _split2(x):
    hi = x.astype(BF16)
    return hi, (x - hi.astype(F32)).astype(BF16)


def _dot3(a, b):
    a_hi, a_lo = _split2(a)
    b_hi, b_lo = _split2(b)
    return _dot(a_hi, b_hi) + _dot(a_lo, b_hi) + _dot(a_hi, b_lo)


def _silu(x):
    return x * jax.nn.sigmoid(x)


def _norm_mod(x, g, shift, scale):
    y = x * lax.rsqrt(jnp.mean(x * x, axis=-1, keepdims=True) + EPS) * g
    return y * (1.0 + scale) + shift


def _split3(x):
    hi = x.astype(BF16)
    r1 = x - hi.astype(F32)
    mid = r1.astype(BF16)
    lo = (r1 - mid.astype(F32)).astype(BF16)
    return hi, mid, lo


def _ada_kernel(c_ref, w_ref, b_ref, o_ref):
    o_ref[0] = _xdot(_silu(c_ref[...]), w_ref[0]) + b_ref[0]


def _ada(cvec, ada_w, ada_b):
    depth, d, n6 = ada_w.shape
    nt = n6 // d
    return pl.pallas_call(
        _ada_kernel,
        grid=(depth, nt),
        in_specs=[pl.BlockSpec((8, d), lambda l, n: (0, 0)),
                  pl.BlockSpec((1, d, d), lambda l, n: (l, 0, n)),
                  pl.BlockSpec((1, 1, d), lambda l, n: (l, 0, n))],
        out_specs=pl.BlockSpec((1, 8, d), lambda l, n: (l, 0, n)),
        out_shape=jax.ShapeDtypeStruct((depth, 8, n6), F32),
        compiler_params=_cparams("arbitrary", "arbitrary"),
    )(cvec, ada_w, ada_b.reshape(depth, 1, n6))


def _gla_in_kernel(x_ref, g_ref, sh_ref, sc_ref, w_ref, w2_ref, b2_ref,
                   q_ref, k_ref, v_ref, r_ref, la_ref, *, dk, dv):
    h = _norm_mod(x_ref[0], g_ref[...], sh_ref[0], sc_ref[0])
    y = _dot(h.astype(BF16), w_ref[...])
    hk = dk // GLA_HEADS
    q_ref[0] = y[:, :dk] * (hk ** -0.5)
    k_ref[0] = y[:, dk:2 * dk]
    v_ref[0] = y[:, 2 * dk:2 * dk + dv]
    r_ref[0] = y[:, 2 * dk + dv:2 * dk + 2 * dv]
    a = y[:, 2 * dk + 2 * dv:]
    z = _dot3(a, w2_ref[...]) + b2_ref[...]
    log_sig = jnp.minimum(z, 0.0) - jnp.log(1.0 + jnp.exp(-jnp.abs(z)))
    la_ref[0] = log_sig * (1.0 / GLA_TAU)


def _gla_in(x, g, shift, scale, w_main, w2, b2, tm):
    b, t, d = x.shape
    dk = w2.shape[1] // 2
    dv = (w_main.shape[1] - LANES - 2 * dk) // 2
    nw = w_main.shape[1]
    tok = lambda width: pl.BlockSpec((1, tm, width), lambda bi, ti: (bi, ti, 0))
    vec = pl.BlockSpec((1, 1, d), lambda bi, ti: (bi, 0, 0))
    full = lambda shape: pl.BlockSpec(shape, lambda bi, ti: tuple(0 for _ in shape))
    return pl.pallas_call(
        functools.partial(_gla_in_kernel, dk=dk, dv=dv),
        grid=(b, t // tm),
        in_specs=[tok(d), full((1, d)), vec, vec, full((d, nw)), full((LANES, 2 * dk)), full((1, 2 * dk))],
        out_specs=[tok(dk), tok(dk), tok(dv), tok(dv), tok(2 * dk)],
        out_shape=[jax.ShapeDtypeStruct((b, t, dk), F32), jax.ShapeDtypeStruct((b, t, dk), F32),
                   jax.ShapeDtypeStruct((b, t, dv), F32), jax.ShapeDtypeStruct((b, t, dv), F32),
                   jax.ShapeDtypeStruct((b, t, 2 * dk), F32)],
        compiler_params=_cparams("arbitrary", "arbitrary"),
    )(x, g, shift, scale, w_main, w2, b2)


def _gla_chunk(q, k, v, la, st, rev):
    c, hk = q.shape
    ii = lax.broadcasted_iota(I32, (c, c), 0)
    jj = lax.broadcasted_iota(I32, (c, c), 1)
    tri = ((jj >= ii) if rev else (jj <= ii)).astype(BF16)
    hi, mid, lo = _split3(la)
    acum = _dot(tri, hi) + _dot(tri, mid) + _dot(tri, lo)
    a_last = acum[0:1] if rev else acum[c - 1:c]

    o = _dot_nt((q * jnp.exp(acum)).astype(BF16), st.astype(BF16))

    xor = ii ^ jj
    rowi = lax.broadcasted_iota(I32, (c, hk), 0)
    p = jnp.zeros((c, c), F32)
    blk = c
    while blk >= 2 * GLA_DIAG:
        half = blk // 2
        a3 = acum.reshape(c // blk, blk, hk)
        ridx = half if rev else half - 1
        ref = jnp.broadcast_to(a3[:, ridx:ridx + 1, :], a3.shape).reshape(c, hk)
        pos = rowi & (blk - 1)
        late = (pos < half) if rev else (pos >= half)
        qe = jnp.where(late, q * jnp.exp(acum - ref), 0.0).astype(BF16)
        ke = jnp.where(late, 0.0, k * jnp.exp(ref - acum)).astype(BF16)
        raw = _dot_nt(qe, ke)
        p = p + (raw if blk == c else jnp.where(xor < blk, raw, 0.0))
        blk = half

    nb = c // GLA_DIAG
    q8 = q.reshape(nb, GLA_DIAG, hk)
    k8 = k.reshape(nb, GLA_DIAG, hk)
    a8 = acum.reshape(nb, GLA_DIAG, hk)
    sub = lax.broadcasted_iota(I32, (nb, GLA_DIAG, hk), 1)
    colmod = jj & (GLA_DIAG - 1)
    pd = jnp.zeros((c, c), F32)
    for j in range(GLA_DIAG):
        kj = jnp.broadcast_to(k8[:, j:j + 1, :], k8.shape)
        aj = jnp.broadcast_to(a8[:, j:j + 1, :], a8.shape)
        valid = (sub <= j) if rev else (sub >= j)
        dec = jnp.where(valid, jnp.exp(a8 - aj), 0.0)
        cj = jnp.sum(q8 * kj * dec, axis=2, keepdims=True).reshape(c, 1)
        pd = jnp.where(colmod == j, cj, pd)
    p = p + jnp.where(xor < GLA_DIAG, pd, 0.0)

    vb = v.astype(BF16)
    o = o + _dot(p.astype(BF16), vb)
    kd = (k * jnp.exp(a_last - acum)).astype(BF16)
    st_new = st * jnp.exp(a_last) + _dot_tn(vb, kd)
    return o, st_new


def _gla_scan_kernel(qf_ref, kf_ref, vf_ref, laf_ref, qb_ref, kb_ref, vb_ref, lab_ref,
                     s0f_ref, s0b_ref, of_ref, ob_ref, sf_ref, sb_ref, stf, stb):
    n = pl.program_id(2)

    @pl.when(n == 0)
    def _():
        stf[...] = s0f_ref[0, 0]
        stb[...] = s0b_ref[0, 0]

    o, s = _gla_chunk(qf_ref[0], kf_ref[0], vf_ref[0], laf_ref[0], stf[...], False)
    of_ref[0] = o
    stf[...] = s
    o, s = _gla_chunk(qb_ref[0], kb_ref[0], vb_ref[0], lab_ref[0], stb[...], True)
    ob_ref[0] = o
    stb[...] = s

    @pl.when(n == pl.num_programs(2) - 1)
    def _():
        sf_ref[0, 0] = stf[...]
        sb_ref[0, 0] = stb[...]


def _gla_scan(q, k, v, la, s0f, s0b):
    b, t, dk = q.shape
    dv = v.shape[2]
    hk, hv = dk // GLA_HEADS, dv // GLA_HEADS
    c = GLA_CHUNK
    nc = t // c
    fwd = lambda w, off: pl.BlockSpec((1, c, w), lambda bi, h, n: (bi, n, h + off))
    bwd = lambda w, off: pl.BlockSpec((1, c, w), lambda bi, h, n: (bi, nc - 1 - n, h + off))
    st = pl.BlockSpec((1, 1, hv, hk), lambda bi, h, n: (bi, h, 0, 0))
    return pl.pallas_call(
        _gla_scan_kernel,
        grid=(b, GLA_HEADS, nc),
        in_specs=[fwd(hk, 0), fwd(hk, 0), fwd(hv, 0), fwd(hk, 0),
                  bwd(hk, 0), bwd(hk, 0), bwd(hv, 0), bwd(hk, GLA_HEADS), st, st],
        out_specs=[fwd(hv, 0), bwd(hv, 0), st, st],
        out_shape=[jax.ShapeDtypeStruct((b, t, dv), F32), jax.ShapeDtypeStruct((b, t, dv), F32),
                   jax.ShapeDtypeStruct((b, GLA_HEADS, hv, hk), F32),
                   jax.ShapeDtypeStruct((b, GLA_HEADS, hv, hk), F32)],
        scratch_shapes=[pltpu.VMEM((hv, hk), F32), pltpu.VMEM((hv, hk), F32)],
        compiler_params=_cparams("arbitrary", "arbitrary", "arbitrary"),
    )(q, k, v, la, q, k, v, la, s0f, s0b)


def _gla_out_kernel(of_ref, ob_ref, r_ref, x_ref, gn_ref, w_ref, g1_ref, o_ref):
    o = of_ref[0] + ob_ref[0]
    hv = gn_ref.shape[1]
    parts = []
    for h in range(GLA_HEADS):
        oh = o[:, h * hv:(h + 1) * hv]
        ms = jnp.mean(oh * oh, axis=-1, keepdims=True)
        parts.append(oh * lax.rsqrt(ms + EPS) * gn_ref[...])
    y = jnp.concatenate(parts, axis=1) * _silu(r_ref[0])
    o_ref[0] = x_ref[0] + g1_ref[0] * _dot(y.astype(BF16), w_ref[...])


def _gla_out(o_f, o_b, r, x, gn, w_out, g1, tm):
    b, t, d = x.shape
    dv = o_f.shape[2]
    tok = lambda width: pl.BlockSpec((1, tm, width), lambda bi, ti: (bi, ti, 0))
    return pl.pallas_call(
        _gla_out_kernel,
        grid=(b, t // tm),
        in_specs=[tok(dv), tok(dv), tok(dv), tok(d),
                  pl.BlockSpec((1, dv // GLA_HEADS), lambda bi, ti: (0, 0)),
                  pl.BlockSpec((dv, d), lambda bi, ti: (0, 0)),
                  pl.BlockSpec((1, 1, d), lambda bi, ti: (bi, 0, 0))],
        out_specs=tok(d),
        out_shape=jax.ShapeDtypeStruct((b, t, d), F32),
        compiler_params=_cparams("arbitrary", "arbitrary"),
    )(o_f, o_b, r, x, gn, w_out, g1)


def _conv_mix_kernel(x_ref, g_ref, sh_ref, sc_ref, win_ref, ck_ref, wout_ref, g1_ref, o_ref, *, seg):
    x = x_ref[0]
    tm, d = x.shape
    h = _norm_mod(x, g_ref[...], sh_ref[0], sc_ref[0])
    y = _dot(h.astype(BF16), win_ref[...])
    bg, cg, v = y[:, :d], y[:, d:2 * d], y[:, 2 * d:]
    u = cg * v
    pos = lax.broadcasted_iota(I32, (tm, 1), 0) % seg
    u_prev = jnp.where(pos == 0, 0.0, pltpu.roll(u, 1, 0))
    u_next = jnp.where(pos == seg - 1, 0.0, pltpu.roll(u, tm - 1, 0))
    conv = u_prev * ck_ref[0:1, :] + u * ck_ref[1:2, :] + u_next * ck_ref[2:3, :]
    o_ref[0] = x + g1_ref[0] * _dot((bg * conv).astype(BF16), wout_ref[...])


def _conv_mix(x, g, shift, scale, w_in, ck, w_out, g1, seg, tm):
    b, t, d = x.shape
    tok = pl.BlockSpec((1, tm, d), lambda bi, ti: (bi, ti, 0))
    vec = pl.BlockSpec((1, 1, d), lambda bi, ti: (bi, 0, 0))
    full = lambda shape: pl.BlockSpec(shape, lambda bi, ti: tuple(0 for _ in shape))
    return pl.pallas_call(
        functools.partial(_conv_mix_kernel, seg=seg),
        grid=(b, t // tm),
        in_specs=[tok, full((1, d)), vec, vec, full((d, 3 * d)), full((CONV_WIDTH, d)), full((d, d)), vec],
        out_specs=tok,
        out_shape=jax.ShapeDtypeStruct((b, t, d), F32),
        compiler_params=_cparams("arbitrary", "arbitrary"),
    )(x, g, shift, scale, w_in, ck, w_out, g1)


def _router_kernel(x_ref, g_ref, sh_ref, sc_ref, wr_ref, h_ref, aff_ref):
    d = x_ref.shape[2]
    h = _norm_mod(x_ref[0], g_ref[...], sh_ref[0], sc_ref[0])
    logits = _dot3(h, wr_ref[...])
    lane = lax.broadcasted_iota(I32, logits.shape, 1)
    is_e = lane < N_EXPERTS
    m = jnp.max(jnp.where(is_e, logits, -jnp.inf), axis=-1, keepdims=True)
    ex = jnp.where(is_e, jnp.exp(logits - m), 0.0)
    aff = ex / jnp.sum(ex, axis=-1, keepdims=True)
    aff_ref[0] = aff.T[:N_EXPERTS, :]
    hi, mid, lo = _split3(aff)
    ext = (hi.astype(F32) + pltpu.roll(mid.astype(F32), N_EXPERTS, 1)
           + pltpu.roll(lo.astype(F32), 2 * N_EXPERTS, 1))
    h_ref[0, :, :d] = h.astype(BF16)
    h_ref[0, :, d:] = ext.astype(BF16)


def _router(x, g, shift, scale, wr_pad, tm):
    b, t, d = x.shape
    tok = lambda width: pl.BlockSpec((1, tm, width), lambda bi, ti: (bi, ti, 0))
    vec = pl.BlockSpec((1, 1, d), lambda bi, ti: (bi, 0, 0))
    return pl.pallas_call(
        _router_kernel,
        grid=(b, t // tm),
        in_specs=[tok(d), pl.BlockSpec((1, d), lambda bi, ti: (0, 0)), vec, vec,
                  pl.BlockSpec((d, LANES), lambda bi, ti: (0, 0))],
        out_specs=[tok(d + LANES), pl.BlockSpec((1, N_EXPERTS, tm), lambda bi, ti: (bi, 0, ti))],
        out_shape=[jax.ShapeDtypeStruct((b, t, d + LANES), BF16),
                   jax.ShapeDtypeStruct((b, N_EXPERTS, t), F32)],
        compiler_params=_cparams("arbitrary", "arbitrary"),
    )(x, g, shift, scale, wr_pad)


def _select_kernel(aff_ref, ridx_ref, table_ref, goff_ref, used_ref, *, cap, tr, capp, pl_lanes):
    e_n = N_EXPERTS
    a = aff_ref[0]
    r_n = a.shape[1]

    def count_ge(v):
        return jnp.sum(jnp.where(a >= v, 1.0, 0.0), axis=(1, 2), keepdims=True)

    def search(it, cur):
        cand = cur | (jnp.int32(1) << (30 - it))
        return jnp.where(count_ge(lax.bitcast_convert_type(cand, F32)) >= cap, cand, cur)

    thr = lax.fori_loop(0, 31, search, jnp.zeros((e_n, 1, 1), I32))
    normal = thr >= MIN_NORMAL_BITS
    lo = lax.bitcast_convert_type(jnp.where(normal, thr, 0), F32)
    hi = lax.bitcast_convert_type(jnp.where(normal, thr + 1, MIN_NORMAL_BITS), F32)

    def refine(it, lh):
        lo_, hi_ = lh
        mid = 0.5 * (lo_ + hi_)
        ok = count_ge(mid) >= cap
        return jnp.where(ok, mid, lo_), jnp.where(ok, hi_, mid)

    lo, hi = lax.fori_loop(0, REFINE_STEPS, refine, (lo, hi))
    gt = a >= hi
    eq = (a >= lo) & jnp.logical_not(gt)
    need = cap - jnp.sum(jnp.where(gt, 1.0, 0.0), axis=(1, 2), keepdims=True)

    li = lax.broadcasted_iota(I32, (LANES, LANES), 0)
    lj = lax.broadcasted_iota(I32, (LANES, LANES), 1)
    upper = (li <= lj).astype(BF16)
    ones = jnp.ones((LANES, LANES), BF16)
    ri = lax.broadcasted_iota(I32, (r_n, r_n), 0)
    rj = lax.broadcasted_iota(I32, (r_n, r_n), 1)
    strict_lower = (rj < ri).astype(BF16)

    def cumsum_tokens(mask):
        m = jnp.where(mask, 1.0, 0.0).astype(BF16).reshape(e_n * r_n, LANES)
        within = _dot(m, upper).reshape(e_n, r_n, LANES)
        rs = _dot(m, ones).reshape(e_n, r_n, LANES)
        pre = jnp.stack([_dot(strict_lower, rs[e].astype(BF16)) for e in range(e_n)])
        return within + pre, rs

    eq_rank, _ = cumsum_tokens(eq)
    sel = gt | (eq & (eq_rank <= need))
    incl, rs = cumsum_tokens(sel)
    pos = incl - jnp.where(sel, 1.0, 0.0)

    same_tile = ((ri // tr) == (rj // tr)).astype(F32)
    prev_tile = ((rj // tr) < (ri // tr)).astype(F32)
    first_row = (rj % tr) == 0
    prev_first = jnp.where(first_row, prev_tile, 0.0)
    all_first = jnp.where(first_row, 1.0, 0.0)

    goff = jnp.zeros((r_n, LANES), F32)
    lane = lax.broadcasted_iota(I32, (r_n, LANES), 1)
    gtab = jnp.zeros((r_n, LANES), F32)
    rows = []
    tbl = jnp.full((r_n, pl_lanes), -1.0, F32)
    row16 = (lax.broadcasted_iota(I32, (r_n, pl_lanes), 1) * ROW_ALIGN).astype(F32)
    rep = lambda x: jnp.concatenate([x] * (pl_lanes // LANES), axis=1)
    base = (pl.program_id(0) * e_n * capp).astype(F32)
    for e in range(e_n):
        n_tile = _xdot(same_tile, rs[e])
        pos_start = _xdot(prev_tile, rs[e])
        n_pad = jnp.floor((n_tile + (ROW_ALIGN - 1)) * (1.0 / ROW_ALIGN)) * ROW_ALIGN
        off = _xdot(prev_first, n_pad)
        used_ref[0, e] = _xdot(all_first, n_pad).astype(I32)
        rows.append(jnp.where(sel[e], goff + pos[e] - pos_start, -1.0))
        gtab = jnp.where(lane == e, goff, gtab)
        g0, g1 = rep(goff), rep(goff + n_pad)
        inside = (row16 >= g0) & (row16 < g1)
        tbl = jnp.where(inside, base + e * capp + rep(off) + row16 - g0, tbl)
        goff = goff + n_pad
    ridx_ref[0] = jnp.stack(rows).astype(I32)
    table_ref[0] = tbl.astype(I32)
    goff_ref[0] = jnp.where(lane >= e_n, goff, gtab).astype(I32)


def _select(aff_r, cap, tr, capp, pl_lanes):
    b, e_n, r_n, _ = aff_r.shape
    blk4 = pl.BlockSpec((1, e_n, r_n, LANES), lambda bi: (bi, 0, 0, 0))
    return pl.pallas_call(
        functools.partial(_select_kernel, cap=cap, tr=tr, capp=capp, pl_lanes=pl_lanes),
        grid=(b,),
        in_specs=[blk4],
        out_specs=[blk4, pl.BlockSpec((1, r_n, pl_lanes), lambda bi: (bi, 0, 0)),
                   pl.BlockSpec((1, r_n, LANES), lambda bi: (bi, 0, 0)), blk4],
        out_shape=[jax.ShapeDtypeStruct((b, e_n, r_n, LANES), I32),
                   jax.ShapeDtypeStruct((b, r_n, pl_lanes), I32),
                   jax.ShapeDtypeStruct((b, r_n, LANES), I32),
                   jax.ShapeDtypeStruct((b, e_n, r_n, LANES), I32)],
        compiler_params=_cparams("arbitrary"),
    )(aff_r)


def _one_hot_rows(ridx_ref, goff_ref, base, tr):
    e_lo = jnp.int32(0)
    e_hi = jnp.int32(0)
    for e in range(N_EXPERTS):
        e_lo += (goff_ref[0, 0, e + 1] <= base).astype(I32)
        e_hi += (goff_ref[0, 0, e] < base + ROW_CHUNK).astype(I32)
    riota = lax.broadcasted_iota(I32, (ROW_CHUNK, LANES), 0) + base
    pieces = []
    for q in range(tr):
        def add_expert(e, acc, q=q):
            return jnp.where(ridx_ref[0, e, 0, pl.ds(q, 1), :] == riota, 1.0, acc)
        acc = lax.fori_loop(e_lo, e_hi, add_expert, jnp.zeros((ROW_CHUNK, LANES), F32))
        pieces.append(acc.astype(BF16))
    return jnp.concatenate(pieces, axis=1)


def _dispatch_kernel(nrows_ref, table_ref, goff_ref, x_ref, ridx_ref, xe_in_ref, xe_ref, stage, sem,
                     *, tr):
    del xe_in_ref
    bi, ti = pl.program_id(0), pl.program_id(1)
    nch = (nrows_ref[bi, ti] + (ROW_CHUNK - 1)) // ROW_CHUNK
    x = x_ref[0]

    def piece(c, slot, p):
        dst = table_ref[0, 0, c * PIECES + p]
        cp = pltpu.make_async_copy(
            stage.at[slot, pl.ds(p * ROW_ALIGN, ROW_ALIGN), :],
            xe_ref.at[pl.ds(pl.multiple_of(jnp.maximum(dst, 0), ROW_ALIGN), ROW_ALIGN), :],
            sem.at[slot])
        return dst, cp

    def start_chunk(c, slot):
        for p in range(PIECES):
            dst, cp = piece(c, slot, p)
            pl.when(dst >= 0)(cp.start)

    def wait_chunk(c, slot):
        for p in range(PIECES):
            dst, cp = piece(c, slot, p)
            pl.when(dst >= 0)(cp.wait)

    def body(c, carry):
        slot = c % 2
        pl.when(c >= 2)(lambda: wait_chunk(c - 2, slot))
        oh = _one_hot_rows(ridx_ref, goff_ref, c * ROW_CHUNK, tr)
        stage[slot] = _dot(oh, x).astype(BF16)
        start_chunk(c, slot)
        return carry

    lax.fori_loop(0, nch, body, 0)
    pl.when(nch >= 2)(lambda: wait_chunk(nch - 2, nch % 2))
    pl.when(nch >= 1)(lambda: wait_chunk(nch - 1, (nch - 1) % 2))


def _dispatch(h_ext, ridx5, table3, goff3, nrows, xe_zero, tt):
    b, t, dx = h_ext.shape
    nt = t // tt
    tr = tt // LANES
    pl_lanes = table3.shape[2]
    grid_spec = pltpu.PrefetchScalarGridSpec(
        num_scalar_prefetch=1,
        grid=(b, nt),
        in_specs=[pl.BlockSpec((1, 1, pl_lanes), lambda bi, ti, nr: (bi * nt + ti, 0, 0),
                               memory_space=pltpu.SMEM),
                  pl.BlockSpec((1, 1, LANES), lambda bi, ti, nr: (bi * nt + ti, 0, 0),
                               memory_space=pltpu.SMEM),
                  pl.BlockSpec((1, tt, dx), lambda bi, ti, nr: (bi, ti, 0)),
                  pl.BlockSpec((1, N_EXPERTS, 1, tr, LANES), lambda bi, ti, nr: (bi, 0, ti, 0, 0)),
                  pl.BlockSpec(memory_space=pl.ANY)],
        out_specs=pl.BlockSpec(memory_space=pl.ANY),
        scratch_shapes=[pltpu.VMEM((2, ROW_CHUNK, dx), BF16), pltpu.SemaphoreType.DMA((2,))],
    )
    return pl.pallas_call(
        functools.partial(_dispatch_kernel, tr=tr),
        grid_spec=grid_spec,
        out_shape=jax.ShapeDtypeStruct(xe_zero.shape, BF16),
        input_output_aliases={5: 0},
        compiler_params=_cparams("arbitrary", "arbitrary"),
    )(nrows, table3, goff3, h_ext, ridx5, xe_zero)


def _ffn_kernel(used_ref, x_ref, wg_ref, wu_ref, wd_ref, o_ref, acc, *, rb):
    e, bi, fi = pl.program_id(0), pl.program_id(1), pl.program_id(2)
    capp, d = o_ref.shape[2], o_ref.shape[3]
    nblk = (used_ref[bi, e] + (rb - 1)) // rb
    wg = wg_ref[0, 0].astype(BF16)
    wu = wu_ref[0, 0].astype(BF16)
    wd = wd_ref[0, 0].astype(BF16)

    def body(j, carry):
        rows = pl.ds(pl.multiple_of(j * rb, rb), rb)
        xm = x_ref[0, 0, rows, :d]
        part = _dot((_silu(_dot(xm, wg)) * _dot(xm, wu)).astype(BF16), wd)

        @pl.when(fi == 0)
        def _():
            acc[rows, :] = part

        @pl.when(fi > 0)
        def _():
            acc[rows, :] += part

        return carry

    lax.fori_loop(0, nblk, body, 0)

    @pl.when(fi == pl.num_programs(2) - 1)
    def _():
        lane = lax.broadcasted_iota(I32, (1, LANES), 1)
        own = (lane == e) | (lane == e + N_EXPERTS) | (lane == e + 2 * N_EXPERTS)

        def scale(j, carry):
            rows = pl.ds(pl.multiple_of(j * rb, rb), rb)
            ext = x_ref[0, 0, rows, d:].astype(F32)
            val = jnp.sum(jnp.where(own, ext, 0.0), axis=1, keepdims=True)
            o_ref[0, 0, rows, :] = (acc[rows, :] * val).astype(BF16)
            return carry

        def zero(j, carry):
            o_ref[0, 0, pl.ds(pl.multiple_of(j * rb, rb), rb), :] = jnp.zeros((rb, d), BF16)
            return carry

        lax.fori_loop(0, nblk, scale, 0)
        lax.fori_loop(nblk, capp // rb, zero, 0)


def _ffn(xe, used, wg, wu, wd, layer, rb):
    b, e_n, capp, dx = xe.shape
    d, f = wg.shape[2], wg.shape[3]
    ft = min(FFN_F_TILE, f)
    grid_spec = pltpu.PrefetchScalarGridSpec(
        num_scalar_prefetch=1,
        grid=(e_n, b, f // ft),
        in_specs=[pl.BlockSpec((1, 1, capp, dx), lambda e, bi, fi, u: (bi, e, 0, 0)),
                  pl.BlockSpec((1, 1, d, ft), lambda e, bi, fi, u: (layer, e, 0, fi)),
                  pl.BlockSpec((1, 1, d, ft), lambda e, bi, fi, u: (layer, e, 0, fi)),
                  pl.BlockSpec((1, 1, ft, d), lambda e, bi, fi, u: (layer, e, fi, 0))],
        out_specs=pl.BlockSpec((1, 1, capp, d), lambda e, bi, fi, u: (bi, e, 0, 0)),
        scratch_shapes=[pltpu.VMEM((capp, d), F32)],
    )
    return pl.pallas_call(
        functools.partial(_ffn_kernel, rb=rb),
        grid_spec=grid_spec,
        out_shape=jax.ShapeDtypeStruct((b, e_n, capp, d), BF16),
        compiler_params=_cparams("arbitrary", "arbitrary", "arbitrary"),
    )(used, xe, wg, wu, wd)


def _combine_kernel(nrows_ref, table_ref, goff_ref, ridx_ref, x_ref, g2_ref, fg_ref, ye_ref, o_ref,
                    stage, acc, sem, *, tr, final):
    bi, ti = pl.program_id(0), pl.program_id(1)
    nch = (nrows_ref[bi, ti] + (ROW_CHUNK - 1)) // ROW_CHUNK

    @pl.when((bi == 0) & (ti == 0))
    def _():
        stage[...] = jnp.zeros(stage.shape, BF16)

    def piece(c, slot, p):
        src = table_ref[0, 0, c * PIECES + p]
        cp = pltpu.make_async_copy(
            ye_ref.at[pl.ds(pl.multiple_of(jnp.maximum(src, 0), ROW_ALIGN), ROW_ALIGN), :],
            stage.at[slot, pl.ds(p * ROW_ALIGN, ROW_ALIGN), :],
            sem.at[slot])
        return src, cp

    def fetch(c, slot):
        for p in range(PIECES):
            src, cp = piece(c, slot, p)
            pl.when(src >= 0)(cp.start)

    def wait(c, slot):
        for p in range(PIECES):
            src, cp = piece(c, slot, p)
            pl.when(src >= 0)(cp.wait)

    acc[...] = jnp.zeros(acc.shape, F32)
    pl.when(nch >= 1)(lambda: fetch(0, 0))

    def body(c, carry):
        slot = c % 2
        pl.when(c + 1 < nch)(lambda: fetch(c + 1, 1 - slot))
        wait(c, slot)
        oh = _one_hot_rows(ridx_ref, goff_ref, c * ROW_CHUNK, tr)
        acc[...] += _dot_tn(oh, stage[slot])
        return carry

    lax.fori_loop(0, nch, body, 0)
    xo = x_ref[0] + g2_ref[0] * acc[...]
    if final:
        xo = xo * lax.rsqrt(jnp.mean(xo * xo, axis=-1, keepdims=True) + EPS) * fg_ref[...]
    o_ref[0] = xo


def _combine(ye_flat, ridx5, table3, goff3, nrows, x, g2, fg, tt, final):
    b, t, d = x.shape
    nt = t // tt
    tr = tt // LANES
    pl_lanes = table3.shape[2]
    grid_spec = pltpu.PrefetchScalarGridSpec(
        num_scalar_prefetch=1,
        grid=(b, nt),
        in_specs=[pl.BlockSpec((1, 1, pl_lanes), lambda bi, ti, nr: (bi * nt + ti, 0, 0),
                               memory_space=pltpu.SMEM),
                  pl.BlockSpec((1, 1, LANES), lambda bi, ti, nr: (bi * nt + ti, 0, 0),
                               memory_space=pltpu.SMEM),
                  pl.BlockSpec((1, N_EXPERTS, 1, tr, LANES), lambda bi, ti, nr: (bi, 0, ti, 0, 0)),
                  pl.BlockSpec((1, tt, d), lambda bi, ti, nr: (bi, ti, 0)),
                  pl.BlockSpec((1, 1, d), lambda bi, ti, nr: (bi, 0, 0)),
                  pl.BlockSpec((1, d), lambda bi, ti, nr: (0, 0)),
                  pl.BlockSpec(memory_space=pl.ANY)],
        out_specs=pl.BlockSpec((1, tt, d), lambda bi, ti, nr: (bi, ti, 0)),
        scratch_shapes=[pltpu.VMEM((2, ROW_CHUNK, d), BF16), pltpu.VMEM((tt, d), F32),
                        pltpu.SemaphoreType.DMA((2,))],
    )
    return pl.pallas_call(
        functools.partial(_combine_kernel, tr=tr, final=final),
        grid_spec=grid_spec,
        out_shape=jax.ShapeDtypeStruct((b, t, d), F32),
        compiler_params=_cparams("arbitrary", "arbitrary"),
    )(nrows, table3, goff3, ridx5, x, g2, fg, ye_flat)


def _moe_tiles(t):
    tt = min(512, t)
    t_sel = max(t, 8 * LANES)
    return tt, t_sel


def _ec_moe(x, g, shift, scale, gate, wr_pad, wg, wu, wd, layer, final_g, final):
    b, t, d = x.shape
    e_n = N_EXPERTS
    cap = EC_CAPACITY_FACTOR * t // e_n
    tt, t_sel = _moe_tiles(t)
    nt, tr = t // tt, tt // LANES
    capp = cap + ROW_ALIGN * nt
    rb = min(ROW_CHUNK, capp)
    capp = -(-capp // rb) * rb
    pl_lanes = -(-(tt + ROW_ALIGN) // LANES) * LANES

    h_ext, aff_t = _router(x, g, shift, scale, wr_pad, min(512, t))
    if t_sel > t:
        aff_t = jnp.pad(aff_t, ((0, 0), (0, 0), (0, t_sel - t)), constant_values=-1.0)
    aff_r = aff_t.reshape(b, e_n, t_sel // LANES, LANES)
    ridx, table, goff, used = _select(aff_r, cap, tr, capp, pl_lanes)
    ridx5 = ridx[:, :, :nt * tr].reshape(b, e_n, nt, tr, LANES)
    table3 = table[:, :nt * tr:tr].reshape(b * nt, 1, pl_lanes)
    goff3 = goff[:, :nt * tr:tr].reshape(b * nt, 1, LANES)
    nrows2 = goff[:, :nt * tr:tr, e_n]
    used2 = used[:, :, 0, 0]

    xe_zero = jnp.zeros((b * e_n * capp, d + LANES), BF16)
    xe = _dispatch(h_ext, ridx5, table3, goff3, nrows2, xe_zero, tt)
    ye = _ffn(xe.reshape(b, e_n, capp, d + LANES), used2, wg, wu, wd, layer, rb)
    return _combine(ye.reshape(b * e_n * capp, d), ridx5, table3, goff3, nrows2, x, gate, final_g, tt,
                    final)


def kernel(x, c, ctx, c_ctx, ada_w, ada_b, norm_g, gla_w_in, gla_w_a2, gla_b_a2, gla_norm_g,
           gla_w_out, conv_w_in, conv_k, conv_w_out, router_w, expert_w_gate, expert_w_up,
           expert_w_down, final_norm_g):
    b, t, d = x.shape
    depth = ada_w.shape[0]
    assert depth == 2 and b + 1 <= 8
    dk = gla_w_a2.shape[3]
    dv = gla_w_out.shape[1]
    hk, hv = dk // GLA_HEADS, dv // GLA_HEADS

    cvec = jnp.zeros((8, d), F32).at[:b].set(c).at[b].set(c_ctx)
    mods = _ada(cvec, ada_w, ada_b)

    def mod_vectors(layer, rows):
        m = mods[layer, rows].reshape(rows.shape[0], 1, 6, d)
        return [m[:, :, i, :] for i in range(6)]

    rows_x = jnp.arange(b)
    rows_c = jnp.full((b,), b)
    wr_pad = [jnp.pad(router_w[i], ((0, 0), (0, LANES - N_EXPERTS))) for i in range(depth)]
    wg, wu, wd = expert_w_gate, expert_w_up, expert_w_down
    fg = final_norm_g.reshape(1, d)

    sh1x, sc1x, g1x, sh2x, sc2x, g2x = mod_vectors(0, rows_x)
    sh1c, sc1c, g1c, sh2c, sc2c, g2c = mod_vectors(0, rows_c)
    n_main = 2 * dk + 2 * dv
    w_main = jnp.pad(gla_w_in[0], ((0, 0), (0, LANES - 2 * GLA_RANK))).astype(BF16)
    assert w_main.shape[1] == n_main + LANES
    w2 = jnp.zeros((LANES, 2 * dk), F32)
    w2 = w2.at[:GLA_RANK, :dk].set(gla_w_a2[0, 0]).at[GLA_RANK:2 * GLA_RANK, dk:].set(gla_w_a2[0, 1])
    b2 = gla_b_a2[0].reshape(1, 2 * dk)
    gn = gla_norm_g[0].reshape(1, hv)
    w_out = gla_w_out[0].astype(BF16)
    g_n1 = norm_g[0, 0].reshape(1, d)
    g_n2 = norm_g[0, 1].reshape(1, d)

    tc = ctx.shape[1]
    qc, kc, vc, rc, lac = _gla_in(ctx, g_n1, sh1c, sc1c, w_main, w2, b2, min(512, tc))
    s0 = jnp.zeros((b, GLA_HEADS, hv, hk), F32)
    ocf, ocb, s_cf, s_cb = _gla_scan(qc, kc, vc, lac, s0, s0)
    qx, kx, vx, rx, lax_ = _gla_in(x, g_n1, sh1x, sc1x, w_main, w2, b2, 512)
    oxf, oxb, _, _ = _gla_scan(qx, kx, vx, lax_, s_cf, s_cb)
    x = _gla_out(oxf, oxb, rx, x, gn, w_out, g1x, 512)
    x = _ec_moe(x, g_n2, sh2x, sc2x, g2x, wr_pad[0], wg, wu, wd, 0, fg, False)

    ctx = _gla_out(ocf, ocb, rc, ctx, gn, w_out, g1c, min(512, tc))
    ctx = _ec_moe(ctx, g_n2, sh2c, sc2c, g2c, wr_pad[0], wg, wu, wd, 0, fg, False)
    del ctx

    sh1x, sc1x, g1x, sh2x, sc2x, g2x = mod_vectors(1, rows_x)
    x = _conv_mix(x, norm_g[1, 0].reshape(1, d), sh1x, sc1x, conv_w_in[0].astype(BF16), conv_k[0],
                  conv_w_out[0].astype(BF16), g1x, GRID_W, 512)
    return _ec_moe(x, norm_g[1, 1].reshape(1, d), sh2x, sc2x, g2x, wr_pad[1], wg, wu, wd, 1, fg, True)
```

```python
import functools

import jax
import jax.numpy as jnp
from jax import lax
from jax.experimental import pallas as pl
from jax.experimental.pallas import tpu as pltpu

F32 = jnp.float32
BF16 = jnp.bfloat16
I32 = jnp.int32
HIGHEST = lax.Precision.HIGHEST

EPS = 1e-6
GRID_W = 64
GLA_HEADS = 4
GLA_RANK = 16
GLA_TAU = 16.0
N_EXPERTS = 16
EC_CAPACITY_FACTOR = 2
CONV_WIDTH = 3

LANES = 128
ROW_ALIGN = 16
GLA_CHUNK = 256
GLA_DIAG = 8
ROW_CHUNK = 256
PIECES = ROW_CHUNK // ROW_ALIGN
ONE_HOT_SLOTS = 3
TRASH_ROWS = 2 * ROW_CHUNK
MIN_NORMAL_BITS = 0x00800000
REFINE_STEPS = 40
VMEM_LIMIT = 60 * 1024 * 1024


def _cparams(*sem):
    return pltpu.CompilerParams(dimension_semantics=sem, vmem_limit_bytes=VMEM_LIMIT)


def _dot(a, b):
    return jnp.dot(a, b, preferred_element_type=F32)


def _dot_nt(a, b):
    return lax.dot_general(a, b, (((1,), (1,)), ((), ())), preferred_element_type=F32)


def _dot_tn(a, b):
    return lax.dot_general(a, b, (((0,), (0,)), ((), ())), preferred_element_type=F32)


def _xdot(a, b):
    return jnp.dot(a, b, precision=HIGHEST, preferred_element_type=F32)


def _split2(x):
    hi = x.astype(BF16)
    return hi, (x - hi.astype(F32)).astype(BF16)


def _dot3(a, b):
    a_hi, a_lo = _split2(a)
    b_hi, b_lo = _split2(b)
    return _dot(a_hi, b_hi) + _dot(a_lo, b_hi) + _dot(a_hi, b_lo)


def _silu(x):
    return x * jax.nn.sigmoid(x)


def _norm_mod(x, g, shift, scale):
    y = x * lax.rsqrt(jnp.mean(x * x, axis=-1, keepdims=True) + EPS) * g
    return y * (1.0 + scale) + shift


def _split3(x):
    hi = x.astype(BF16)
    r1 = x - hi.astype(F32)
    mid = r1.astype(BF16)
    lo = (r1 - mid.astype(F32)).astype(BF16)
    return hi, mid, lo


def _ada_kernel(c_ref, w_ref, b_ref, o_ref):
    o_ref[0] = _xdot(_silu(c_ref[...]), w_ref[0]) + b_ref[0]


def _ada(cvec, ada_w, ada_b):
    depth, d, n6 = ada_w.shape
    nt = n6 // d
    return pl.pallas_call(
        _ada_kernel,
        grid=(depth, nt),
        in_specs=[pl.BlockSpec((8, d), lambda l, n: (0, 0)),
                  pl.BlockSpec((1, d, d), lambda l, n: (l, 0, n)),
                  pl.BlockSpec((1, 1, d), lambda l, n: (l, 0, n))],
        out_specs=pl.BlockSpec((1, 8, d), lambda l, n: (l, 0, n)),
        out_shape=jax.ShapeDtypeStruct((depth, 8, n6), F32),
        compiler_params=_cparams("arbitrary", "arbitrary"),
    )(cvec, ada_w, ada_b.reshape(depth, 1, n6))


def _gla_in_kernel(x_ref, g_ref, sh_ref, sc_ref, w_ref, w2_ref, b2_ref,
                   q_ref, k_ref, v_ref, r_ref, la_ref, *, dk, dv):
    h = _norm_mod(x_ref[0], g_ref[...], sh_ref[0], sc_ref[0])
    y = _dot(h.astype(BF16), w_ref[...])
    hk = dk // GLA_HEADS
    q_ref[0] = y[:, :dk] * (hk ** -0.5)
    k_ref[0] = y[:, dk:2 * dk]
    v_ref[0] = y[:, 2 * dk:2 * dk + dv]
    r_ref[0] = y[:, 2 * dk + dv:2 * dk + 2 * dv].astype(BF16)
    a = y[:, 2 * dk + 2 * dv:]
    z = _dot3(a, w2_ref[...]) + b2_ref[...]
    log_sig = jnp.minimum(z, 0.0) - jnp.log(1.0 + jnp.exp(-jnp.abs(z)))
    la_ref[0] = log_sig * (1.0 / GLA_TAU)


def _gla_in(x, g, shift, scale, w_main, w2, b2, tm):
    b, t, d = x.shape
    dk = w2.shape[1] // 2
    dv = (w_main.shape[1] - LANES - 2 * dk) // 2
    nw = w_main.shape[1]
    tok = lambda width: pl.BlockSpec((1, tm, width), lambda bi, ti: (bi, ti, 0))
    vec = pl.BlockSpec((1, 1, d), lambda bi, ti: (bi, 0, 0))
    full = lambda shape: pl.BlockSpec(shape, lambda bi, ti: tuple(0 for _ in shape))
    return pl.pallas_call(
        functools.partial(_gla_in_kernel, dk=dk, dv=dv),
        grid=(b, t // tm),
        in_specs=[tok(d), full((1, d)), vec, vec, full((d, nw)), full((LANES, 2 * dk)), full((1, 2 * dk))],
        out_specs=[tok(dk), tok(dk), tok(dv), tok(dv), tok(2 * dk)],
        out_shape=[jax.ShapeDtypeStruct((b, t, dk), F32), jax.ShapeDtypeStruct((b, t, dk), F32),
                   jax.ShapeDtypeStruct((b, t, dv), F32), jax.ShapeDtypeStruct((b, t, dv), BF16),
                   jax.ShapeDtypeStruct((b, t, 2 * dk), F32)],
        compiler_params=_cparams("arbitrary", "arbitrary"),
    )(x, g, shift, scale, w_main, w2, b2)


def _gla_chunk(q, k, v, la, st, rev):
    c, hk = q.shape
    ii = lax.broadcasted_iota(I32, (c, c), 0)
    jj = lax.broadcasted_iota(I32, (c, c), 1)
    tri = ((jj >= ii) if rev else (jj <= ii)).astype(BF16)
    hi, mid, lo = _split3(la)
    acum = _dot(tri, hi) + _dot(tri, mid) + _dot(tri, lo)
    a_last = acum[0:1] if rev else acum[c - 1:c]

    o = _dot_nt((q * jnp.exp(acum)).astype(BF16), st.astype(BF16))

    xor = ii ^ jj
    rowi = lax.broadcasted_iota(I32, (c, hk), 0)
    p = jnp.zeros((c, c), F32)
    blk = c
    while blk >= 2 * GLA_DIAG:
        half = blk // 2
        a3 = acum.reshape(c // blk, blk, hk)
        ridx = half if rev else half - 1
        ref = jnp.broadcast_to(a3[:, ridx:ridx + 1, :], a3.shape).reshape(c, hk)
        pos = rowi & (blk - 1)
        late = (pos < half) if rev else (pos >= half)
        qe = jnp.where(late, q * jnp.exp(acum - ref), 0.0).astype(BF16)
        ke = jnp.where(late, 0.0, k * jnp.exp(ref - acum)).astype(BF16)
        raw = _dot_nt(qe, ke)
        p = p + (raw if blk == c else jnp.where(xor < blk, raw, 0.0))
        blk = half

    nb = c // GLA_DIAG
    q8 = q.reshape(nb, GLA_DIAG, hk)
    k8 = k.reshape(nb, GLA_DIAG, hk)
    a8 = acum.reshape(nb, GLA_DIAG, hk)
    sub = lax.broadcasted_iota(I32, (nb, GLA_DIAG, hk), 1)
    colmod = jj & (GLA_DIAG - 1)
    pd = jnp.zeros((c, c), F32)
    for j in range(GLA_DIAG):
        kj = jnp.broadcast_to(k8[:, j:j + 1, :], k8.shape)
        aj = jnp.broadcast_to(a8[:, j:j + 1, :], a8.shape)
        valid = (sub <= j) if rev else (sub >= j)
        dec = jnp.where(valid, jnp.exp(a8 - aj), 0.0)
        cj = jnp.sum(q8 * kj * dec, axis=2, keepdims=True).reshape(c, 1)
        pd = jnp.where(colmod == j, cj, pd)
    p = p + jnp.where(xor < GLA_DIAG, pd, 0.0)

    vb = v.astype(BF16)
    o = o + _dot(p.astype(BF16), vb)
    kd = (k * jnp.exp(a_last - acum)).astype(BF16)
    st_new = st * jnp.exp(a_last) + _dot_tn(vb, kd)
    return o, st_new


def _gla_scan_kernel(qf_ref, kf_ref, vf_ref, laf_ref, qb_ref, kb_ref, vb_ref, lab_ref,
                     s0f_ref, s0b_ref, of_ref, ob_ref, sf_ref, sb_ref, stf, stb):
    n = pl.program_id(2)

    @pl.when(n == 0)
    def _():
        stf[...] = s0f_ref[0, 0]
        stb[...] = s0b_ref[0, 0]

    o, s = _gla_chunk(qf_ref[0], kf_ref[0], vf_ref[0], laf_ref[0], stf[...], False)
    of_ref[0] = o.astype(BF16)
    stf[...] = s
    o, s = _gla_chunk(qb_ref[0], kb_ref[0], vb_ref[0], lab_ref[0], stb[...], True)
    ob_ref[0] = o.astype(BF16)
    stb[...] = s

    @pl.when(n == pl.num_programs(2) - 1)
    def _():
        sf_ref[0, 0] = stf[...]
        sb_ref[0, 0] = stb[...]


def _gla_scan(q, k, v, la, s0f, s0b):
    b, t, dk = q.shape
    dv = v.shape[2]
    hk, hv = dk // GLA_HEADS, dv // GLA_HEADS
    c = GLA_CHUNK
    nc = t // c
    fwd = lambda w, off: pl.BlockSpec((1, c, w), lambda bi, h, n: (bi, n, h + off))
    bwd = lambda w, off: pl.BlockSpec((1, c, w), lambda bi, h, n: (bi, nc - 1 - n, h + off))
    st = pl.BlockSpec((1, 1, hv, hk), lambda bi, h, n: (bi, h, 0, 0))
    return pl.pallas_call(
        _gla_scan_kernel,
        grid=(b, GLA_HEADS, nc),
        in_specs=[fwd(hk, 0), fwd(hk, 0), fwd(hv, 0), fwd(hk, 0),
                  bwd(hk, 0), bwd(hk, 0), bwd(hv, 0), bwd(hk, GLA_HEADS), st, st],
        out_specs=[fwd(hv, 0), bwd(hv, 0), st, st],
        out_shape=[jax.ShapeDtypeStruct((b, t, dv), BF16), jax.ShapeDtypeStruct((b, t, dv), BF16),
                   jax.ShapeDtypeStruct((b, GLA_HEADS, hv, hk), F32),
                   jax.ShapeDtypeStruct((b, GLA_HEADS, hv, hk), F32)],
        scratch_shapes=[pltpu.VMEM((hv, hk), F32), pltpu.VMEM((hv, hk), F32)],
        compiler_params=_cparams("arbitrary", "arbitrary", "arbitrary"),
    )(q, k, v, la, q, k, v, la, s0f, s0b)


def _gla_out_kernel(of_ref, ob_ref, r_ref, x_ref, gn_ref, w_ref, g1_ref, o_ref):
    o = of_ref[0].astype(F32) + ob_ref[0].astype(F32)
    hv = gn_ref.shape[1]
    parts = []
    for h in range(GLA_HEADS):
        oh = o[:, h * hv:(h + 1) * hv]
        ms = jnp.mean(oh * oh, axis=-1, keepdims=True)
        parts.append(oh * lax.rsqrt(ms + EPS) * gn_ref[...])
    y = jnp.concatenate(parts, axis=1) * _silu(r_ref[0].astype(F32))
    o_ref[0] = x_ref[0] + g1_ref[0] * _dot(y.astype(BF16), w_ref[...])


def _gla_out(o_f, o_b, r, x, gn, w_out, g1, tm):
    b, t, d = x.shape
    dv = o_f.shape[2]
    tok = lambda width: pl.BlockSpec((1, tm, width), lambda bi, ti: (bi, ti, 0))
    return pl.pallas_call(
        _gla_out_kernel,
        grid=(b, t // tm),
        in_specs=[tok(dv), tok(dv), tok(dv), tok(d),
                  pl.BlockSpec((1, dv // GLA_HEADS), lambda bi, ti: (0, 0)),
                  pl.BlockSpec((dv, d), lambda bi, ti: (0, 0)),
                  pl.BlockSpec((1, 1, d), lambda bi, ti: (bi, 0, 0))],
        out_specs=tok(d),
        out_shape=jax.ShapeDtypeStruct((b, t, d), F32),
        compiler_params=_cparams("arbitrary", "arbitrary"),
    )(o_f, o_b, r, x, gn, w_out, g1)


def _conv_mix_kernel(x_ref, g_ref, sh_ref, sc_ref, win_ref, ck_ref, wout_ref, g1_ref, o_ref, *, seg):
    x = x_ref[0]
    tm, d = x.shape
    h = _norm_mod(x, g_ref[...], sh_ref[0], sc_ref[0])
    y = _dot(h.astype(BF16), win_ref[...])
    bg, cg, v = y[:, :d], y[:, d:2 * d], y[:, 2 * d:]
    u = cg * v
    pos = lax.broadcasted_iota(I32, (tm, 1), 0) % seg
    u_prev = jnp.where(pos == 0, 0.0, pltpu.roll(u, 1, 0))
    u_next = jnp.where(pos == seg - 1, 0.0, pltpu.roll(u, tm - 1, 0))
    conv = u_prev * ck_ref[0:1, :] + u * ck_ref[1:2, :] + u_next * ck_ref[2:3, :]
    o_ref[0] = x + g1_ref[0] * _dot((bg * conv).astype(BF16), wout_ref[...])


def _conv_mix(x, g, shift, scale, w_in, ck, w_out, g1, seg, tm):
    b, t, d = x.shape
    tok = pl.BlockSpec((1, tm, d), lambda bi, ti: (bi, ti, 0))
    vec = pl.BlockSpec((1, 1, d), lambda bi, ti: (bi, 0, 0))
    full = lambda shape: pl.BlockSpec(shape, lambda bi, ti: tuple(0 for _ in shape))
    return pl.pallas_call(
        functools.partial(_conv_mix_kernel, seg=seg),
        grid=(b, t // tm),
        in_specs=[tok, full((1, d)), vec, vec, full((d, 3 * d)), full((CONV_WIDTH, d)), full((d, d)), vec],
        out_specs=tok,
        out_shape=jax.ShapeDtypeStruct((b, t, d), F32),
        compiler_params=_cparams("arbitrary", "arbitrary"),
    )(x, g, shift, scale, w_in, ck, w_out, g1)


def _router_kernel(x_ref, g_ref, sh_ref, sc_ref, wr_ref, h_ref, aff_ref):
    d = x_ref.shape[2]
    h = _norm_mod(x_ref[0], g_ref[...], sh_ref[0], sc_ref[0])
    logits = _dot3(h, wr_ref[...])
    lane = lax.broadcasted_iota(I32, logits.shape, 1)
    is_e = lane < N_EXPERTS
    m = jnp.max(jnp.where(is_e, logits, -jnp.inf), axis=-1, keepdims=True)
    ex = jnp.where(is_e, jnp.exp(logits - m), 0.0)
    aff = ex / jnp.sum(ex, axis=-1, keepdims=True)
    aff_ref[0] = aff.T[:N_EXPERTS, :]
    hi, mid, lo = _split3(aff)
    ext = (hi.astype(F32) + pltpu.roll(mid.astype(F32), N_EXPERTS, 1)
           + pltpu.roll(lo.astype(F32), 2 * N_EXPERTS, 1))
    h_ref[0, :, :d] = h.astype(BF16)
    h_ref[0, :, d:] = ext.astype(BF16)


def _router(x, g, shift, scale, wr_pad, tm):
    b, t, d = x.shape
    tok = lambda width: pl.BlockSpec((1, tm, width), lambda bi, ti: (bi, ti, 0))
    vec = pl.BlockSpec((1, 1, d), lambda bi, ti: (bi, 0, 0))
    return pl.pallas_call(
        _router_kernel,
        grid=(b, t // tm),
        in_specs=[tok(d), pl.BlockSpec((1, d), lambda bi, ti: (0, 0)), vec, vec,
                  pl.BlockSpec((d, LANES), lambda bi, ti: (0, 0))],
        out_specs=[tok(d + LANES), pl.BlockSpec((1, N_EXPERTS, tm), lambda bi, ti: (bi, 0, ti))],
        out_shape=[jax.ShapeDtypeStruct((b, t, d + LANES), BF16),
                   jax.ShapeDtypeStruct((b, N_EXPERTS, t), F32)],
        compiler_params=_cparams("arbitrary", "arbitrary"),
    )(x, g, shift, scale, wr_pad)


def _select_kernel(aff_ref, ridx_ref, table_ref, goff_ref, used_ref, *, cap, tr, capp, pl_lanes):
    e_n = N_EXPERTS
    a = aff_ref[0]
    r_n = a.shape[1]

    def count_ge(v):
        return jnp.sum(jnp.where(a >= v, 1.0, 0.0), axis=(1, 2), keepdims=True)

    def search(it, cur):
        cand = cur | (jnp.int32(1) << (30 - it))
        return jnp.where(count_ge(lax.bitcast_convert_type(cand, F32)) >= cap, cand, cur)

    thr = lax.fori_loop(0, 31, search, jnp.zeros((e_n, 1, 1), I32))
    normal = thr >= MIN_NORMAL_BITS
    lo = lax.bitcast_convert_type(jnp.where(normal, thr, 0), F32)
    hi = lax.bitcast_convert_type(jnp.where(normal, thr + 1, MIN_NORMAL_BITS), F32)

    def refine(it, lh):
        lo_, hi_ = lh
        mid = 0.5 * (lo_ + hi_)
        ok = count_ge(mid) >= cap
        return jnp.where(ok, mid, lo_), jnp.where(ok, hi_, mid)

    lo, hi = lax.fori_loop(0, REFINE_STEPS, refine, (lo, hi))
    gt = a >= hi
    eq = (a >= lo) & jnp.logical_not(gt)
    need = cap - jnp.sum(jnp.where(gt, 1.0, 0.0), axis=(1, 2), keepdims=True)

    li = lax.broadcasted_iota(I32, (LANES, LANES), 0)
    lj = lax.broadcasted_iota(I32, (LANES, LANES), 1)
    upper = (li <= lj).astype(BF16)
    ones = jnp.ones((LANES, LANES), BF16)
    ri = lax.broadcasted_iota(I32, (r_n, r_n), 0)
    rj = lax.broadcasted_iota(I32, (r_n, r_n), 1)
    strict_lower = (rj < ri).astype(BF16)

    def cumsum_tokens(mask):
        m = jnp.where(mask, 1.0, 0.0).astype(BF16).reshape(e_n * r_n, LANES)
        within = _dot(m, upper).reshape(e_n, r_n, LANES)
        rs = _dot(m, ones).reshape(e_n, r_n, LANES)
        pre = jnp.stack([_dot(strict_lower, rs[e].astype(BF16)) for e in range(e_n)])
        return within + pre, rs

    eq_rank, _ = cumsum_tokens(eq)
    sel = gt | (eq & (eq_rank <= need))
    incl, rs = cumsum_tokens(sel)
    pos = incl - jnp.where(sel, 1.0, 0.0)

    same_tile = ((ri // tr) == (rj // tr)).astype(F32)
    prev_tile = ((rj // tr) < (ri // tr)).astype(F32)
    first_row = (rj % tr) == 0
    prev_first = jnp.where(first_row, prev_tile, 0.0)
    all_first = jnp.where(first_row, 1.0, 0.0)

    goff = jnp.zeros((r_n, LANES), F32)
    lane = lax.broadcasted_iota(I32, (r_n, LANES), 1)
    gtab = jnp.zeros((r_n, LANES), F32)
    rows = []
    tbl = jnp.full((r_n, pl_lanes), -1.0, F32)
    row16 = (lax.broadcasted_iota(I32, (r_n, pl_lanes), 1) * ROW_ALIGN).astype(F32)
    rep = lambda x: jnp.concatenate([x] * (pl_lanes // LANES), axis=1)
    base = (pl.program_id(0) * e_n * capp).astype(F32)
    for e in range(e_n):
        n_tile = _xdot(same_tile, rs[e])
        pos_start = _xdot(prev_tile, rs[e])
        n_pad = jnp.floor((n_tile + (ROW_ALIGN - 1)) * (1.0 / ROW_ALIGN)) * ROW_ALIGN
        off = _xdot(prev_first, n_pad)
        used_ref[0, e] = _xdot(all_first, n_pad).astype(I32)
        rows.append(jnp.where(sel[e], goff + pos[e] - pos_start, -1.0))
        gtab = jnp.where(lane == e, goff, gtab)
        g0, g1 = rep(goff), rep(goff + n_pad)
        inside = (row16 >= g0) & (row16 < g1)
        tbl = jnp.where(inside, base + e * capp + rep(off) + row16 - g0, tbl)
        goff = goff + n_pad
    ridx_ref[0] = jnp.stack(rows).astype(I32)
    table_ref[0] = tbl.astype(I32)
    goff_ref[0] = jnp.where(lane >= e_n, goff, gtab).astype(I32)


def _select(aff_r, cap, tr, capp, pl_lanes):
    b, e_n, r_n, _ = aff_r.shape
    blk4 = pl.BlockSpec((1, e_n, r_n, LANES), lambda bi: (bi, 0, 0, 0))
    return pl.pallas_call(
        functools.partial(_select_kernel, cap=cap, tr=tr, capp=capp, pl_lanes=pl_lanes),
        grid=(b,),
        in_specs=[blk4],
        out_specs=[blk4, pl.BlockSpec((1, r_n, pl_lanes), lambda bi: (bi, 0, 0)),
                   pl.BlockSpec((1, r_n, LANES), lambda bi: (bi, 0, 0)), blk4],
        out_shape=[jax.ShapeDtypeStruct((b, e_n, r_n, LANES), I32),
                   jax.ShapeDtypeStruct((b, r_n, pl_lanes), I32),
                   jax.ShapeDtypeStruct((b, r_n, LANES), I32),
                   jax.ShapeDtypeStruct((b, e_n, r_n, LANES), I32)],
        compiler_params=_cparams("arbitrary"),
    )(aff_r)


def _expert_span(goff_ref, base):
    e_lo = jnp.int32(0)
    e_hi = jnp.int32(0)
    for e in range(N_EXPERTS):
        e_lo += (goff_ref[0, 0, e + 1] <= base).astype(I32)
        e_hi += (goff_ref[0, 0, e] < base + ROW_CHUNK).astype(I32)
    return e_lo, e_hi


def _one_hot_rows(ridx_ref, base, e_lo, e_hi, tr):
    riota = lax.broadcasted_iota(I32, (ROW_CHUNK, LANES), 0) + base
    pieces = []
    for q in range(tr):
        acc = jnp.zeros((ROW_CHUNK, LANES), F32)
        for k in range(ONE_HOT_SLOTS):
            e = jnp.minimum(e_lo + k, N_EXPERTS - 1)
            row = jnp.where(e_lo + k < e_hi, ridx_ref[0, e, 0, pl.ds(q, 1), :], -2)
            acc = jnp.where(row == riota, 1.0, acc)
        pieces.append(acc.astype(BF16))
    return jnp.concatenate(pieces, axis=1)


def _one_hot_rows_any(ridx_ref, base, e_lo, e_hi, tr):
    riota = lax.broadcasted_iota(I32, (ROW_CHUNK, LANES), 0) + base
    pieces = []
    for q in range(tr):
        def add_expert(e, acc, q=q):
            return jnp.where(ridx_ref[0, e, 0, pl.ds(q, 1), :] == riota, 1.0, acc)
        acc = lax.fori_loop(e_lo, e_hi, add_expert, jnp.zeros((ROW_CHUNK, LANES), F32))
        pieces.append(acc.astype(BF16))
    return jnp.concatenate(pieces, axis=1)


def _build_one_hot(oh_ref, slot, ridx_ref, goff_ref, c, tr):
    base = c * ROW_CHUNK
    e_lo, e_hi = _expert_span(goff_ref, base)
    oh_ref[slot] = _one_hot_rows(ridx_ref, base, e_lo, e_hi, tr)
    return e_lo, e_hi


def _patch_one_hot(oh_ref, slot, ridx_ref, c, e_lo, e_hi, tr):
    @pl.when(e_hi - e_lo > ONE_HOT_SLOTS)
    def _():
        oh_ref[slot] = _one_hot_rows_any(ridx_ref, c * ROW_CHUNK, e_lo, e_hi, tr)


def _dispatch_kernel(nrows_ref, used_ref, table_ref, goff_ref, x_ref, ridx_ref, xe_ref,
                     stage, oh, zeros, sem, zsem, *, tr, capp):
    bi, ti = pl.program_id(0), pl.program_id(1)
    nch = (nrows_ref[bi, ti] + (ROW_CHUNK - 1)) // ROW_CHUNK
    x = x_ref[0]
    trash = pl.num_programs(0) * N_EXPERTS * capp

    def piece(c, slot, p):
        dst = table_ref[0, 0, c * PIECES + p]
        dst = jnp.where(dst >= 0, dst, trash + (slot * PIECES + p) * ROW_ALIGN)
        return pltpu.make_async_copy(
            stage.at[slot, pl.ds(p * ROW_ALIGN, ROW_ALIGN), :],
            xe_ref.at[pl.ds(pl.multiple_of(dst, ROW_ALIGN), ROW_ALIGN), :],
            sem.at[slot])

    def start_chunk(c, slot):
        for p in range(PIECES):
            piece(c, slot, p).start()

    def wait_chunk(c, slot):
        for p in range(PIECES):
            piece(c, slot, p).wait()

    def sort_chunk(c, slot):
        span = _build_one_hot(oh, 1 - slot, ridx_ref, goff_ref, c + 1, tr)
        stage[slot] = _dot(oh[slot], x).astype(BF16)
        _patch_one_hot(oh, 1 - slot, ridx_ref, c + 1, *span, tr)

    span = _build_one_hot(oh, 0, ridx_ref, goff_ref, 0, tr)
    _patch_one_hot(oh, 0, ridx_ref, 0, *span, tr)
    pl.when(nch >= 1)(lambda: sort_chunk(0, 0))

    def body(c, carry):
        slot = c % 2
        pl.when(c >= 2)(lambda: wait_chunk(c - 2, slot))
        start_chunk(c - 1, 1 - slot)
        sort_chunk(c, slot)
        return carry

    lax.fori_loop(1, nch, body, 0)

    @pl.when(nch >= 1)
    def _():
        start_chunk(nch - 1, (nch - 1) % 2)

    pl.when(nch >= 2)(lambda: wait_chunk(nch - 2, nch % 2))
    pl.when(nch >= 1)(lambda: wait_chunk(nch - 1, (nch - 1) % 2))

    def zero_piece(row):
        return pltpu.make_async_copy(
            zeros, xe_ref.at[pl.ds(pl.multiple_of(row, ROW_ALIGN), ROW_ALIGN), :], zsem)

    def zero_rows(first, n_pieces):
        def start(k, carry):
            zero_piece(first + k * ROW_ALIGN).start()
            return carry

        def wait(k, carry):
            zero_piece(first + k * ROW_ALIGN).wait()
            return carry

        lax.fori_loop(0, n_pieces, start, 0)
        lax.fori_loop(0, n_pieces, wait, 0)

    @pl.when(ti == pl.num_programs(1) - 1)
    def _():
        zeros[...] = jnp.zeros(zeros.shape, BF16)
        for e in range(N_EXPERTS):
            used = used_ref[bi, e]
            zero_rows((bi * N_EXPERTS + e) * capp + used, (capp - used) // ROW_ALIGN)

        @pl.when(bi == pl.num_programs(0) - 1)
        def _():
            zero_rows(trash, TRASH_ROWS // ROW_ALIGN)


def _dispatch(h_ext, ridx5, table3, goff3, nrows, used, capp, tt):
    b, t, dx = h_ext.shape
    nt = t // tt
    tr = tt // LANES
    pl_lanes = table3.shape[2]
    grid_spec = pltpu.PrefetchScalarGridSpec(
        num_scalar_prefetch=2,
        grid=(b, nt),
        in_specs=[pl.BlockSpec((1, 1, pl_lanes), lambda bi, ti, nr, us: (bi * nt + ti, 0, 0),
                               memory_space=pltpu.SMEM),
                  pl.BlockSpec((1, 1, LANES), lambda bi, ti, nr, us: (bi * nt + ti, 0, 0),
                               memory_space=pltpu.SMEM),
                  pl.BlockSpec((1, tt, dx), lambda bi, ti, nr, us: (bi, ti, 0)),
                  pl.BlockSpec((1, N_EXPERTS, 1, tr, LANES), lambda bi, ti, nr, us: (bi, 0, ti, 0, 0))],
        out_specs=pl.BlockSpec(memory_space=pl.ANY),
        scratch_shapes=[pltpu.VMEM((2, ROW_CHUNK, dx), BF16), pltpu.VMEM((2, ROW_CHUNK, tt), BF16),
                        pltpu.VMEM((ROW_ALIGN, dx), BF16),
                        pltpu.SemaphoreType.DMA((2,)), pltpu.SemaphoreType.DMA(())],
    )
    return pl.pallas_call(
        functools.partial(_dispatch_kernel, tr=tr, capp=capp),
        grid_spec=grid_spec,
        out_shape=jax.ShapeDtypeStruct((b * N_EXPERTS * capp + TRASH_ROWS, dx), BF16),
        compiler_params=_cparams("arbitrary", "arbitrary"),
    )(nrows, used, table3, goff3, h_ext, ridx5)


def _ffn_kernel(used_ref, x_ref, wg_ref, wu_ref, wd_ref, o_ref, *, rb):
    e, bi = pl.program_id(0), pl.program_id(1)
    capp, d = o_ref.shape[2], o_ref.shape[3]
    nblk = (used_ref[bi, e] + (rb - 1)) // rb
    lane = lax.broadcasted_iota(I32, (1, LANES), 1)
    own = (lane == e) | (lane == e + N_EXPERTS) | (lane == e + 2 * N_EXPERTS)

    def body(j, carry):
        rows = pl.ds(pl.multiple_of(j * rb, rb), rb)
        xb = x_ref[rows, :]
        xm = xb[:, :d]
        val = jnp.sum(jnp.where(own, xb[:, d:].astype(F32), 0.0), axis=1, keepdims=True)
        hid = (_silu(_dot(xm, wg_ref[0, 0])) * _dot(xm, wu_ref[0, 0])).astype(BF16)
        o_ref[0, 0, rows, :] = (_dot(hid, wd_ref[0, 0]) * val).astype(BF16)
        return carry

    def zero(j, carry):
        o_ref[0, 0, pl.ds(pl.multiple_of(j * rb, rb), rb), :] = jnp.zeros((rb, d), BF16)
        return carry

    lax.fori_loop(0, nblk, body, 0)
    lax.fori_loop(nblk, capp // rb, zero, 0)


def _ffn(xe, used, wg, wu, wd, layer, capp, rb):
    b, e_n = used.shape
    dx = xe.shape[1]
    d, f = wg.shape[2], wg.shape[3]
    grid_spec = pltpu.PrefetchScalarGridSpec(
        num_scalar_prefetch=1,
        grid=(e_n, b),
        in_specs=[pl.BlockSpec((capp, dx), lambda e, bi, u: (bi * e_n + e, 0)),
                  pl.BlockSpec((1, 1, d, f), lambda e, bi, u: (layer, e, 0, 0)),
                  pl.BlockSpec((1, 1, d, f), lambda e, bi, u: (layer, e, 0, 0)),
                  pl.BlockSpec((1, 1, f, d), lambda e, bi, u: (layer, e, 0, 0))],
        out_specs=pl.BlockSpec((1, 1, capp, d), lambda e, bi, u: (bi, e, 0, 0)),
    )
    return pl.pallas_call(
        functools.partial(_ffn_kernel, rb=rb),
        grid_spec=grid_spec,
        out_shape=jax.ShapeDtypeStruct((b, e_n, capp, d), BF16),
        compiler_params=_cparams("arbitrary", "arbitrary"),
    )(used, xe, wg, wu, wd)


def _combine_kernel(nrows_ref, table_ref, goff_ref, ridx_ref, x_ref, g2_ref, fg_ref, ye_ref, o_ref,
                    stage, oh, acc, sem, *, tr, final):
    bi, ti = pl.program_id(0), pl.program_id(1)
    nch = (nrows_ref[bi, ti] + (ROW_CHUNK - 1)) // ROW_CHUNK

    def piece(c, slot, p):
        src = table_ref[0, 0, c * PIECES + p]
        return pltpu.make_async_copy(
            ye_ref.at[pl.ds(pl.multiple_of(jnp.maximum(src, 0), ROW_ALIGN), ROW_ALIGN), :],
            stage.at[slot, pl.ds(p * ROW_ALIGN, ROW_ALIGN), :],
            sem.at[slot])

    def fetch(c, slot):
        for p in range(PIECES):
            piece(c, slot, p).start()

    def wait(c, slot):
        for p in range(PIECES):
            piece(c, slot, p).wait()

    acc[...] = jnp.zeros(acc.shape, F32)
    fetch(0, 0)
    span = _build_one_hot(oh, 0, ridx_ref, goff_ref, 0, tr)
    _patch_one_hot(oh, 0, ridx_ref, 0, *span, tr)

    def body(c, carry):
        slot = c % 2
        wait(c, slot)
        fetch(c + 1, 1 - slot)
        span = _build_one_hot(oh, 1 - slot, ridx_ref, goff_ref, c + 1, tr)
        acc[...] += _dot_tn(oh[slot], stage[slot])
        _patch_one_hot(oh, 1 - slot, ridx_ref, c + 1, *span, tr)
        return carry

    lax.fori_loop(0, nch, body, 0)
    wait(nch, nch % 2)
    xo = x_ref[0] + g2_ref[0] * acc[...]
    if final:
        xo = xo * lax.rsqrt(jnp.mean(xo * xo, axis=-1, keepdims=True) + EPS) * fg_ref[...]
    o_ref[0] = xo


def _combine(ye_flat, ridx5, table3, goff3, nrows, x, g2, fg, tt, final):
    b, t, d = x.shape
    nt = t // tt
    tr = tt // LANES
    pl_lanes = table3.shape[2]
    grid_spec = pltpu.PrefetchScalarGridSpec(
        num_scalar_prefetch=1,
        grid=(b, nt),
        in_specs=[pl.BlockSpec((1, 1, pl_lanes), lambda bi, ti, nr: (bi * nt + ti, 0, 0),
                               memory_space=pltpu.SMEM),
                  pl.BlockSpec((1, 1, LANES), lambda bi, ti, nr: (bi * nt + ti, 0, 0),
                               memory_space=pltpu.SMEM),
                  pl.BlockSpec((1, N_EXPERTS, 1, tr, LANES), lambda bi, ti, nr: (bi, 0, ti, 0, 0)),
                  pl.BlockSpec((1, tt, d), lambda bi, ti, nr: (bi, ti, 0)),
                  pl.BlockSpec((1, 1, d), lambda bi, ti, nr: (bi, 0, 0)),
                  pl.BlockSpec((1, d), lambda bi, ti, nr: (0, 0)),
                  pl.BlockSpec(memory_space=pl.ANY)],
        out_specs=pl.BlockSpec((1, tt, d), lambda bi, ti, nr: (bi, ti, 0)),
        scratch_shapes=[pltpu.VMEM((2, ROW_CHUNK, d), BF16), pltpu.VMEM((2, ROW_CHUNK, tt), BF16),
                        pltpu.VMEM((tt, d), F32), pltpu.SemaphoreType.DMA((2,))],
    )
    return pl.pallas_call(
        functools.partial(_combine_kernel, tr=tr, final=final),
        grid_spec=grid_spec,
        out_shape=jax.ShapeDtypeStruct((b, t, d), F32),
        compiler_params=_cparams("arbitrary", "arbitrary"),
    )(nrows, table3, goff3, ridx5, x, g2, fg, ye_flat)


def _moe_tiles(t):
    tt = min(512, t)
    t_sel = max(t, 8 * LANES)
    return tt, t_sel


def _ec_moe(x, g, shift, scale, gate, wr_pad, wg, wu, wd, layer, final_g, final):
    b, t, d = x.shape
    e_n = N_EXPERTS
    cap = EC_CAPACITY_FACTOR * t // e_n
    tt, t_sel = _moe_tiles(t)
    nt, tr = t // tt, tt // LANES
    capp = cap + ROW_ALIGN * nt
    rb = min(ROW_CHUNK, capp)
    capp = -(-capp // rb) * rb
    pl_lanes = -(-(tt + ROW_ALIGN) // LANES) * LANES

    h_ext, aff_t = _router(x, g, shift, scale, wr_pad, min(512, t))
    if t_sel > t:
        aff_t = jnp.pad(aff_t, ((0, 0), (0, 0), (0, t_sel - t)), constant_values=-1.0)
    aff_r = aff_t.reshape(b, e_n, t_sel // LANES, LANES)
    ridx, table, goff, used = _select(aff_r, cap, tr, capp, pl_lanes)
    ridx5 = ridx[:, :, :nt * tr].reshape(b, e_n, nt, tr, LANES)
    table3 = table[:, :nt * tr:tr].reshape(b * nt, 1, pl_lanes)
    goff3 = goff[:, :nt * tr:tr].reshape(b * nt, 1, LANES)
    nrows2 = goff[:, :nt * tr:tr, e_n]
    used2 = used[:, :, 0, 0]

    xe = _dispatch(h_ext, ridx5, table3, goff3, nrows2, used2, capp, tt)
    ye = _ffn(xe, used2, wg, wu, wd, layer, capp, rb)
    return _combine(ye.reshape(b * e_n * capp, d), ridx5, table3, goff3, nrows2, x, gate, final_g, tt,
                    final)


def kernel(x, c, ctx, c_ctx, ada_w, ada_b, norm_g, gla_w_in, gla_w_a2, gla_b_a2, gla_norm_g,
           gla_w_out, conv_w_in, conv_k, conv_w_out, router_w, expert_w_gate, expert_w_up,
           expert_w_down, final_norm_g):
    b, t, d = x.shape
    depth = ada_w.shape[0]
    assert depth == 2 and b + 1 <= 8
    dk = gla_w_a2.shape[3]
    dv = gla_w_out.shape[1]
    hk, hv = dk // GLA_HEADS, dv // GLA_HEADS

    cvec = jnp.zeros((8, d), F32).at[:b].set(c).at[b].set(c_ctx)
    mods = _ada(cvec, ada_w, ada_b)

    def mod_vectors(layer, rows):
        m = mods[layer, rows].reshape(rows.shape[0], 1, 6, d)
        return [m[:, :, i, :] for i in range(6)]

    rows_x = jnp.arange(b)
    rows_c = jnp.full((b,), b)
    wr_pad = [jnp.pad(router_w[i], ((0, 0), (0, LANES - N_EXPERTS))) for i in range(depth)]
    wg, wu, wd = (w.astype(BF16) for w in (expert_w_gate, expert_w_up, expert_w_down))
    fg = final_norm_g.reshape(1, d)

    sh1x, sc1x, g1x, sh2x, sc2x, g2x = mod_vectors(0, rows_x)
    sh1c, sc1c, g1c, sh2c, sc2c, g2c = mod_vectors(0, rows_c)
    n_main = 2 * dk + 2 * dv
    w_main = jnp.pad(gla_w_in[0], ((0, 0), (0, LANES - 2 * GLA_RANK))).astype(BF16)
    assert w_main.shape[1] == n_main + LANES
    w2 = jnp.zeros((LANES, 2 * dk), F32)
    w2 = w2.at[:GLA_RANK, :dk].set(gla_w_a2[0, 0]).at[GLA_RANK:2 * GLA_RANK, dk:].set(gla_w_a2[0, 1])
    b2 = gla_b_a2[0].reshape(1, 2 * dk)
    gn = gla_norm_g[0].reshape(1, hv)
    w_out = gla_w_out[0].astype(BF16)
    g_n1 = norm_g[0, 0].reshape(1, d)
    g_n2 = norm_g[0, 1].reshape(1, d)

    tc = ctx.shape[1]
    qc, kc, vc, rc, lac = _gla_in(ctx, g_n1, sh1c, sc1c, w_main, w2, b2, min(512, tc))
    s0 = jnp.zeros((b, GLA_HEADS, hv, hk), F32)
    ocf, ocb, s_cf, s_cb = _gla_scan(qc, kc, vc, lac, s0, s0)
    qx, kx, vx, rx, lax_ = _gla_in(x, g_n1, sh1x, sc1x, w_main, w2, b2, 512)
    oxf, oxb, _, _ = _gla_scan(qx, kx, vx, lax_, s_cf, s_cb)
    x = _gla_out(oxf, oxb, rx, x, gn, w_out, g1x, 512)
    x = _ec_moe(x, g_n2, sh2x, sc2x, g2x, wr_pad[0], wg, wu, wd, 0, fg, False)

    ctx = _gla_out(ocf, ocb, rc, ctx, gn, w_out, g1c, min(512, tc))
    ctx = _ec_moe(ctx, g_n2, sh2c, sc2c, g2c, wr_pad[0], wg, wu, wd, 0, fg, False)
    del ctx

    sh1x, sc1x, g1x, sh2x, sc2x, g2x = mod_vectors(1, rows_x)
    x = _conv_mix(x, norm_g[1, 0].reshape(1, d), sh1x, sc1x, conv_w_in[0].astype(BF16), conv_k[0],
                  conv_w_out[0].astype(BF16), g1x, GRID_W, 512)
    return _ec_moe(x, norm_g[1, 1].reshape(1, d), sh2x, sc2x, g2x, wr_pad[1], wg, wu, wd, 1, fg, True)
```

```python
import functools

import jax
import jax.numpy as jnp
from jax import lax
from jax.experimental import pallas as pl
from jax.experimental.pallas import tpu as pltpu

F32 = jnp.float32
BF16 = jnp.bfloat16
I32 = jnp.int32
HIGHEST = lax.Precision.HIGHEST

EPS = 1e-6
GRID_W = 64
GLA_HEADS = 4
GLA_RANK = 16
GLA_TAU = 16.0
N_EXPERTS = 16
EC_CAPACITY_FACTOR = 2
CONV_WIDTH = 3

LANES = 128
ROW_ALIGN = 16
GLA_CHUNK = 256
GLA_DIAG = 8
ROW_CHUNK = 256
PIECES = ROW_CHUNK // ROW_ALIGN
ONE_HOT_SLOTS = 3
STAGE_SLOTS = 6
TRASH_ROWS = STAGE_SLOTS * ROW_CHUNK
COMBINE_AHEAD = 6
MIN_NORMAL_BITS = 0x00800000
REFINE_STEPS = 40
VMEM_LIMIT = 60 * 1024 * 1024


def _cparams(*sem):
    return pltpu.CompilerParams(dimension_semantics=sem, vmem_limit_bytes=VMEM_LIMIT)


def _dot(a, b):
    return jnp.dot(a, b, preferred_element_type=F32)


def _dot_nt(a, b):
    return lax.dot_general(a, b, (((1,), (1,)), ((), ())), preferred_element_type=F32)


def _dot_tn(a, b):
    return lax.dot_general(a, b, (((0,), (0,)), ((), ())), preferred_element_type=F32)


def _xdot(a, b):
    return jnp.dot(a, b, precision=HIGHEST, preferred_element_type=F32)


def _split2(x):
    hi = x.astype(BF16)
    return hi, (x - hi.astype(F32)).astype(BF16)


def _dot3(a, b):
    a_hi, a_lo = _split2(a)
    b_hi, b_lo = _split2(b)
    return _dot(a_hi, b_hi) + _dot(a_lo, b_hi) + _dot(a_hi, b_lo)


def _silu(x):
    return x * jax.nn.sigmoid(x)


def _norm_mod(x, g, shift, scale):
    y = x * lax.rsqrt(jnp.mean(x * x, axis=-1, keepdims=True) + EPS) * g
    return y * (1.0 + scale) + shift


def _split3(x):
    hi = x.astype(BF16)
    r1 = x - hi.astype(F32)
    mid = r1.astype(BF16)
    lo = (r1 - mid.astype(F32)).astype(BF16)
    return hi, mid, lo


def _ada_kernel(c_ref, w_ref, b_ref, o_ref):
    o_ref[0] = _xdot(_silu(c_ref[...]), w_ref[0]) + b_ref[0]


def _ada(cvec, ada_w, ada_b):
    depth, d, n6 = ada_w.shape
    nt = n6 // d
    return pl.pallas_call(
        _ada_kernel,
        grid=(depth, nt),
        in_specs=[pl.BlockSpec((8, d), lambda l, n: (0, 0)),
                  pl.BlockSpec((1, d, d), lambda l, n: (l, 0, n)),
                  pl.BlockSpec((1, 1, d), lambda l, n: (l, 0, n))],
        out_specs=pl.BlockSpec((1, 8, d), lambda l, n: (l, 0, n)),
        out_shape=jax.ShapeDtypeStruct((depth, 8, n6), F32),
        compiler_params=_cparams("arbitrary", "arbitrary"),
    )(cvec, ada_w, ada_b.reshape(depth, 1, n6))


def _gla_in_kernel(x_ref, g_ref, sh_ref, sc_ref, w_ref, w2_ref, b2_ref,
                   q_ref, k_ref, v_ref, r_ref, la_ref, *, dk, dv):
    h = _norm_mod(x_ref[0], g_ref[...], sh_ref[0], sc_ref[0])
    y = _dot(h.astype(BF16), w_ref[...])
    hk = dk // GLA_HEADS
    q_ref[0] = y[:, :dk] * (hk ** -0.5)
    k_ref[0] = y[:, dk:2 * dk]
    v_ref[0] = y[:, 2 * dk:2 * dk + dv]
    r_ref[0] = y[:, 2 * dk + dv:2 * dk + 2 * dv].astype(BF16)
    a = y[:, 2 * dk + 2 * dv:]
    z = _dot3(a, w2_ref[...]) + b2_ref[...]
    log_sig = jnp.minimum(z, 0.0) - jnp.log(1.0 + jnp.exp(-jnp.abs(z)))
    la_ref[0] = log_sig * (1.0 / GLA_TAU)


def _gla_in(x, g, shift, scale, w_main, w2, b2, tm):
    b, t, d = x.shape
    dk = w2.shape[1] // 2
    dv = (w_main.shape[1] - LANES - 2 * dk) // 2
    nw = w_main.shape[1]
    tok = lambda width: pl.BlockSpec((1, tm, width), lambda bi, ti: (bi, ti, 0))
    vec = pl.BlockSpec((1, 1, d), lambda bi, ti: (bi, 0, 0))
    full = lambda shape: pl.BlockSpec(shape, lambda bi, ti: tuple(0 for _ in shape))
    return pl.pallas_call(
        functools.partial(_gla_in_kernel, dk=dk, dv=dv),
        grid=(b, t // tm),
        in_specs=[tok(d), full((1, d)), vec, vec, full((d, nw)), full((LANES, 2 * dk)), full((1, 2 * dk))],
        out_specs=[tok(dk), tok(dk), tok(dv), tok(dv), tok(2 * dk)],
        out_shape=[jax.ShapeDtypeStruct((b, t, dk), F32), jax.ShapeDtypeStruct((b, t, dk), F32),
                   jax.ShapeDtypeStruct((b, t, dv), F32), jax.ShapeDtypeStruct((b, t, dv), BF16),
                   jax.ShapeDtypeStruct((b, t, 2 * dk), F32)],
        compiler_params=_cparams("arbitrary", "arbitrary"),
    )(x, g, shift, scale, w_main, w2, b2)


def _gla_chunk(q, k, v, la, st, rev):
    c, hk = q.shape
    ii = lax.broadcasted_iota(I32, (c, c), 0)
    jj = lax.broadcasted_iota(I32, (c, c), 1)
    tri = ((jj >= ii) if rev else (jj <= ii)).astype(BF16)
    hi, mid, lo = _split3(la)
    acum = _dot(tri, hi) + _dot(tri, mid) + _dot(tri, lo)
    a_last = acum[0:1] if rev else acum[c - 1:c]

    o = _dot_nt((q * jnp.exp(acum)).astype(BF16), st.astype(BF16))

    xor = ii ^ jj
    rowi = lax.broadcasted_iota(I32, (c, hk), 0)
    p = jnp.zeros((c, c), F32)
    blk = c
    while blk >= 2 * GLA_DIAG:
        half = blk // 2
        a3 = acum.reshape(c // blk, blk, hk)
        ridx = half if rev else half - 1
        ref = jnp.broadcast_to(a3[:, ridx:ridx + 1, :], a3.shape).reshape(c, hk)
        pos = rowi & (blk - 1)
        late = (pos < half) if rev else (pos >= half)
        qe = jnp.where(late, q * jnp.exp(acum - ref), 0.0).astype(BF16)
        ke = jnp.where(late, 0.0, k * jnp.exp(ref - acum)).astype(BF16)
        raw = _dot_nt(qe, ke)
        p = p + (raw if blk == c else jnp.where(xor < blk, raw, 0.0))
        blk = half

    nb = c // GLA_DIAG
    q8 = q.reshape(nb, GLA_DIAG, hk)
    k8 = k.reshape(nb, GLA_DIAG, hk)
    a8 = acum.reshape(nb, GLA_DIAG, hk)
    sub = lax.broadcasted_iota(I32, (nb, GLA_DIAG, hk), 1)
    colmod = jj & (GLA_DIAG - 1)
    pd = jnp.zeros((c, c), F32)
    for j in range(GLA_DIAG):
        kj = jnp.broadcast_to(k8[:, j:j + 1, :], k8.shape)
        aj = jnp.broadcast_to(a8[:, j:j + 1, :], a8.shape)
        valid = (sub <= j) if rev else (sub >= j)
        dec = jnp.where(valid, jnp.exp(a8 - aj), 0.0)
        cj = jnp.sum(q8 * kj * dec, axis=2, keepdims=True).reshape(c, 1)
        pd = jnp.where(colmod == j, cj, pd)
    p = p + jnp.where(xor < GLA_DIAG, pd, 0.0)

    vb = v.astype(BF16)
    o = o + _dot(p.astype(BF16), vb)
    kd = (k * jnp.exp(a_last - acum)).astype(BF16)
    st_new = st * jnp.exp(a_last) + _dot_tn(vb, kd)
    return o, st_new


def _gla_scan_kernel(*refs, n_cast):
    (qf_ref, kf_ref, vf_ref, laf_ref, qb_ref, kb_ref, vb_ref, lab_ref, s0f_ref, s0b_ref) = refs[:10]
    w_in = refs[10:10 + n_cast]
    of_ref, ob_ref, sf_ref, sb_ref = refs[10 + n_cast:14 + n_cast]
    w_out = refs[14 + n_cast:14 + 2 * n_cast]
    stf, stb = refs[14 + 2 * n_cast:]
    n = pl.program_id(2)
    for wi, wo in zip(w_in, w_out):
        wo[0] = wi[0, 0].astype(BF16)

    @pl.when(n == 0)
    def _():
        stf[...] = s0f_ref[0, 0]
        stb[...] = s0b_ref[0, 0]

    o, s = _gla_chunk(qf_ref[0], kf_ref[0], vf_ref[0], laf_ref[0], stf[...], False)
    of_ref[0] = o.astype(BF16)
    stf[...] = s
    o, s = _gla_chunk(qb_ref[0], kb_ref[0], vb_ref[0], lab_ref[0], stb[...], True)
    ob_ref[0] = o.astype(BF16)
    stb[...] = s

    @pl.when(n == pl.num_programs(2) - 1)
    def _():
        sf_ref[0, 0] = stf[...]
        sb_ref[0, 0] = stb[...]


def _gla_scan(q, k, v, la, s0f, s0b, cast=()):
    b, t, dk = q.shape
    dv = v.shape[2]
    hk, hv = dk // GLA_HEADS, dv // GLA_HEADS
    c = GLA_CHUNK
    nc = t // c
    steps = b * GLA_HEADS * nc
    fwd = lambda w, off: pl.BlockSpec((1, c, w), lambda bi, h, n: (bi, n, h + off))
    bwd = lambda w, off: pl.BlockSpec((1, c, w), lambda bi, h, n: (bi, nc - 1 - n, h + off))
    st = pl.BlockSpec((1, 1, hv, hk), lambda bi, h, n: (bi, h, 0, 0))
    cast_in, cast_out, cast_shape = [], [], []
    for w in cast:
        n_l, n_e, rows, cols = w.shape
        per = steps // (n_l * n_e)
        rb = rows // per
        assert per * n_l * n_e == steps and rb * per == rows and rb % ROW_ALIGN == 0
        step = lambda bi, h, n: (bi * GLA_HEADS + h) * nc + n
        cast_in.append(pl.BlockSpec(
            (1, 1, rb, cols),
            lambda bi, h, n, per=per, n_e=n_e: (step(bi, h, n) // per // n_e, step(bi, h, n) // per % n_e,
                                                step(bi, h, n) % per, 0)))
        cast_out.append(pl.BlockSpec(
            (1, rb, cols), lambda bi, h, n, per=per: (step(bi, h, n) // per, step(bi, h, n) % per, 0)))
        cast_shape.append(jax.ShapeDtypeStruct((n_l * n_e, rows, cols), BF16))
    return pl.pallas_call(
        functools.partial(_gla_scan_kernel, n_cast=len(cast)),
        grid=(b, GLA_HEADS, nc),
        in_specs=[fwd(hk, 0), fwd(hk, 0), fwd(hv, 0), fwd(hk, 0),
                  bwd(hk, 0), bwd(hk, 0), bwd(hv, 0), bwd(hk, GLA_HEADS), st, st] + cast_in,
        out_specs=[fwd(hv, 0), bwd(hv, 0), st, st] + cast_out,
        out_shape=[jax.ShapeDtypeStruct((b, t, dv), BF16), jax.ShapeDtypeStruct((b, t, dv), BF16),
                   jax.ShapeDtypeStruct((b, GLA_HEADS, hv, hk), F32),
                   jax.ShapeDtypeStruct((b, GLA_HEADS, hv, hk), F32)] + cast_shape,
        scratch_shapes=[pltpu.VMEM((hv, hk), F32), pltpu.VMEM((hv, hk), F32)],
        compiler_params=_cparams("arbitrary", "arbitrary", "arbitrary"),
    )(q, k, v, la, q, k, v, la, s0f, s0b, *cast)


def _gla_out_kernel(of_ref, ob_ref, r_ref, x_ref, gn_ref, w_ref, g1_ref, o_ref):
    o = of_ref[0].astype(F32) + ob_ref[0].astype(F32)
    hv = gn_ref.shape[1]
    parts = []
    for h in range(GLA_HEADS):
        oh = o[:, h * hv:(h + 1) * hv]
        ms = jnp.mean(oh * oh, axis=-1, keepdims=True)
        parts.append(oh * lax.rsqrt(ms + EPS) * gn_ref[...])
    y = jnp.concatenate(parts, axis=1) * _silu(r_ref[0].astype(F32))
    o_ref[0] = x_ref[0] + g1_ref[0] * _dot(y.astype(BF16), w_ref[...])


def _gla_out(o_f, o_b, r, x, gn, w_out, g1, tm):
    b, t, d = x.shape
    dv = o_f.shape[2]
    tok = lambda width: pl.BlockSpec((1, tm, width), lambda bi, ti: (bi, ti, 0))
    return pl.pallas_call(
        _gla_out_kernel,
        grid=(b, t // tm),
        in_specs=[tok(dv), tok(dv), tok(dv), tok(d),
                  pl.BlockSpec((1, dv // GLA_HEADS), lambda bi, ti: (0, 0)),
                  pl.BlockSpec((dv, d), lambda bi, ti: (0, 0)),
                  pl.BlockSpec((1, 1, d), lambda bi, ti: (bi, 0, 0))],
        out_specs=tok(d),
        out_shape=jax.ShapeDtypeStruct((b, t, d), F32),
        compiler_params=_cparams("arbitrary", "arbitrary"),
    )(o_f, o_b, r, x, gn, w_out, g1)


def _conv_mix_kernel(x_ref, g_ref, sh_ref, sc_ref, win_ref, ck_ref, wout_ref, g1_ref, o_ref, *, seg):
    x = x_ref[0]
    tm, d = x.shape
    h = _norm_mod(x, g_ref[...], sh_ref[0], sc_ref[0])
    y = _dot(h.astype(BF16), win_ref[...])
    bg, cg, v = y[:, :d], y[:, d:2 * d], y[:, 2 * d:]
    u = cg * v
    pos = lax.broadcasted_iota(I32, (tm, 1), 0) % seg
    u_prev = jnp.where(pos == 0, 0.0, pltpu.roll(u, 1, 0))
    u_next = jnp.where(pos == seg - 1, 0.0, pltpu.roll(u, tm - 1, 0))
    conv = u_prev * ck_ref[0:1, :] + u * ck_ref[1:2, :] + u_next * ck_ref[2:3, :]
    o_ref[0] = x + g1_ref[0] * _dot((bg * conv).astype(BF16), wout_ref[...])


def _conv_mix(x, g, shift, scale, w_in, ck, w_out, g1, seg, tm):
    b, t, d = x.shape
    tok = pl.BlockSpec((1, tm, d), lambda bi, ti: (bi, ti, 0))
    vec = pl.BlockSpec((1, 1, d), lambda bi, ti: (bi, 0, 0))
    full = lambda shape: pl.BlockSpec(shape, lambda bi, ti: tuple(0 for _ in shape))
    return pl.pallas_call(
        functools.partial(_conv_mix_kernel, seg=seg),
        grid=(b, t // tm),
        in_specs=[tok, full((1, d)), vec, vec, full((d, 3 * d)), full((CONV_WIDTH, d)), full((d, d)), vec],
        out_specs=tok,
        out_shape=jax.ShapeDtypeStruct((b, t, d), F32),
        compiler_params=_cparams("arbitrary", "arbitrary"),
    )(x, g, shift, scale, w_in, ck, w_out, g1)


def _router_kernel(x_ref, g_ref, sh_ref, sc_ref, wr_ref, h_ref, aff_ref):
    d = x_ref.shape[2]
    h = _norm_mod(x_ref[0], g_ref[...], sh_ref[0], sc_ref[0])
    logits = _dot3(h, wr_ref[...])
    lane = lax.broadcasted_iota(I32, logits.shape, 1)
    is_e = lane < N_EXPERTS
    m = jnp.max(jnp.where(is_e, logits, -jnp.inf), axis=-1, keepdims=True)
    ex = jnp.where(is_e, jnp.exp(logits - m), 0.0)
    aff = ex / jnp.sum(ex, axis=-1, keepdims=True)
    aff_ref[0] = aff.T[:N_EXPERTS, :]
    hi, mid, lo = _split3(aff)
    ext = (hi.astype(F32) + pltpu.roll(mid.astype(F32), N_EXPERTS, 1)
           + pltpu.roll(lo.astype(F32), 2 * N_EXPERTS, 1))
    h_ref[0, :, :d] = h.astype(BF16)
    h_ref[0, :, d:] = ext.astype(BF16)


def _router(x, g, shift, scale, wr_pad, tm):
    b, t, d = x.shape
    tok = lambda width: pl.BlockSpec((1, tm, width), lambda bi, ti: (bi, ti, 0))
    vec = pl.BlockSpec((1, 1, d), lambda bi, ti: (bi, 0, 0))
    return pl.pallas_call(
        _router_kernel,
        grid=(b, t // tm),
        in_specs=[tok(d), pl.BlockSpec((1, d), lambda bi, ti: (0, 0)), vec, vec,
                  pl.BlockSpec((d, LANES), lambda bi, ti: (0, 0))],
        out_specs=[tok(d + LANES), pl.BlockSpec((1, N_EXPERTS, tm), lambda bi, ti: (bi, 0, ti))],
        out_shape=[jax.ShapeDtypeStruct((b, t, d + LANES), BF16),
                   jax.ShapeDtypeStruct((b, N_EXPERTS, t), F32)],
        compiler_params=_cparams("arbitrary", "arbitrary"),
    )(x, g, shift, scale, wr_pad)


def _select_kernel(aff_ref, ridx_ref, table_ref, goff_ref, used_ref, *, cap, tr, capp, pl_lanes):
    e_n = N_EXPERTS
    a = aff_ref[0]
    r_n = a.shape[1]

    def count_ge(v):
        return jnp.sum(jnp.where(a >= v, 1.0, 0.0), axis=(1, 2), keepdims=True)

    def search(it, cur):
        cand = cur | (jnp.int32(1) << (30 - it))
        return jnp.where(count_ge(lax.bitcast_convert_type(cand, F32)) >= cap, cand, cur)

    thr = lax.fori_loop(0, 31, search, jnp.zeros((e_n, 1, 1), I32))
    normal = thr >= MIN_NORMAL_BITS
    lo = lax.bitcast_convert_type(jnp.where(normal, thr, 0), F32)
    hi = lax.bitcast_convert_type(jnp.where(normal, thr + 1, MIN_NORMAL_BITS), F32)

    def refine(it, lh):
        lo_, hi_ = lh
        mid = 0.5 * (lo_ + hi_)
        ok = count_ge(mid) >= cap
        return jnp.where(ok, mid, lo_), jnp.where(ok, hi_, mid)

    lo, hi = lax.fori_loop(0, REFINE_STEPS, refine, (lo, hi))
    gt = a >= hi
    eq = (a >= lo) & jnp.logical_not(gt)
    need = cap - jnp.sum(jnp.where(gt, 1.0, 0.0), axis=(1, 2), keepdims=True)

    li = lax.broadcasted_iota(I32, (LANES, LANES), 0)
    lj = lax.broadcasted_iota(I32, (LANES, LANES), 1)
    upper = (li <= lj).astype(BF16)
    ones = jnp.ones((LANES, LANES), BF16)
    ri = lax.broadcasted_iota(I32, (r_n, r_n), 0)
    rj = lax.broadcasted_iota(I32, (r_n, r_n), 1)
    strict_lower = (rj < ri).astype(BF16)

    def cumsum_tokens(mask):
        m = jnp.where(mask, 1.0, 0.0).astype(BF16).reshape(e_n * r_n, LANES)
        within = _dot(m, upper).reshape(e_n, r_n, LANES)
        rs = _dot(m, ones).reshape(e_n, r_n, LANES)
        pre = jnp.stack([_dot(strict_lower, rs[e].astype(BF16)) for e in range(e_n)])
        return within + pre, rs

    eq_rank, _ = cumsum_tokens(eq)
    sel = gt | (eq & (eq_rank <= need))
    incl, rs = cumsum_tokens(sel)
    pos = incl - jnp.where(sel, 1.0, 0.0)

    same_tile = ((ri // tr) == (rj // tr)).astype(F32)
    prev_tile = ((rj // tr) < (ri // tr)).astype(F32)
    first_row = (rj % tr) == 0
    prev_first = jnp.where(first_row, prev_tile, 0.0)
    all_first = jnp.where(first_row, 1.0, 0.0)

    goff = jnp.zeros((r_n, LANES), F32)
    lane = lax.broadcasted_iota(I32, (r_n, LANES), 1)
    gtab = jnp.zeros((r_n, LANES), F32)
    rows = []
    tbl = jnp.full((r_n, pl_lanes), -1.0, F32)
    row16 = (lax.broadcasted_iota(I32, (r_n, pl_lanes), 1) * ROW_ALIGN).astype(F32)
    rep = lambda x: jnp.concatenate([x] * (pl_lanes // LANES), axis=1)
    base = (pl.program_id(0) * e_n * capp).astype(F32)
    for e in range(e_n):
        n_tile = _xdot(same_tile, rs[e])
        pos_start = _xdot(prev_tile, rs[e])
        n_pad = jnp.floor((n_tile + (ROW_ALIGN - 1)) * (1.0 / ROW_ALIGN)) * ROW_ALIGN
        off = _xdot(prev_first, n_pad)
        used_ref[0, e] = _xdot(all_first, n_pad).astype(I32)
        rows.append(jnp.where(sel[e], goff + pos[e] - pos_start, -1.0))
        gtab = jnp.where(lane == e, goff, gtab)
        g0, g1 = rep(goff), rep(goff + n_pad)
        inside = (row16 >= g0) & (row16 < g1)
        tbl = jnp.where(inside, base + e * capp + rep(off) + row16 - g0, tbl)
        goff = goff + n_pad
    ridx_ref[0] = jnp.stack(rows).astype(I32)
    table_ref[0] = tbl.astype(I32)
    goff_ref[0] = jnp.where(lane >= e_n, goff, gtab).astype(I32)


def _select(aff_r, cap, tr, capp, pl_lanes):
    b, e_n, r_n, _ = aff_r.shape
    blk4 = pl.BlockSpec((1, e_n, r_n, LANES), lambda bi: (bi, 0, 0, 0))
    return pl.pallas_call(
        functools.partial(_select_kernel, cap=cap, tr=tr, capp=capp, pl_lanes=pl_lanes),
        grid=(b,),
        in_specs=[blk4],
        out_specs=[blk4, pl.BlockSpec((1, r_n, pl_lanes), lambda bi: (bi, 0, 0)),
                   pl.BlockSpec((1, r_n, LANES), lambda bi: (bi, 0, 0)), blk4],
        out_shape=[jax.ShapeDtypeStruct((b, e_n, r_n, LANES), I32),
                   jax.ShapeDtypeStruct((b, r_n, pl_lanes), I32),
                   jax.ShapeDtypeStruct((b, r_n, LANES), I32),
                   jax.ShapeDtypeStruct((b, e_n, r_n, LANES), I32)],
        compiler_params=_cparams("arbitrary"),
    )(aff_r)


def _expert_span(goff_ref, base):
    e_lo = jnp.int32(0)
    e_hi = jnp.int32(0)
    for e in range(N_EXPERTS):
        e_lo += (goff_ref[0, 0, e + 1] <= base).astype(I32)
        e_hi += (goff_ref[0, 0, e] < base + ROW_CHUNK).astype(I32)
    return e_lo, e_hi


def _one_hot_rows(ridx_ref, base, e_lo, e_hi, tr):
    riota = lax.broadcasted_iota(I32, (ROW_CHUNK, LANES), 0) + base
    pieces = []
    for q in range(tr):
        acc = jnp.zeros((ROW_CHUNK, LANES), F32)
        for k in range(ONE_HOT_SLOTS):
            e = jnp.minimum(e_lo + k, N_EXPERTS - 1)
            row = jnp.where(e_lo + k < e_hi, ridx_ref[0, e, 0, pl.ds(q, 1), :], -2)
            acc = jnp.where(row == riota, 1.0, acc)
        pieces.append(acc.astype(BF16))
    return jnp.concatenate(pieces, axis=1)


def _one_hot_rows_any(ridx_ref, base, e_lo, e_hi, tr):
    riota = lax.broadcasted_iota(I32, (ROW_CHUNK, LANES), 0) + base
    pieces = []
    for q in range(tr):
        def add_expert(e, acc, q=q):
            return jnp.where(ridx_ref[0, e, 0, pl.ds(q, 1), :] == riota, 1.0, acc)
        acc = lax.fori_loop(e_lo, e_hi, add_expert, jnp.zeros((ROW_CHUNK, LANES), F32))
        pieces.append(acc.astype(BF16))
    return jnp.concatenate(pieces, axis=1)


def _build_one_hot(oh_ref, slot, ridx_ref, goff_ref, c, tr):
    base = c * ROW_CHUNK
    e_lo, e_hi = _expert_span(goff_ref, base)
    oh_ref[slot] = _one_hot_rows(ridx_ref, base, e_lo, e_hi, tr)
    return e_lo, e_hi


def _patch_one_hot(oh_ref, slot, ridx_ref, c, e_lo, e_hi, tr):
    @pl.when(e_hi - e_lo > ONE_HOT_SLOTS)
    def _():
        oh_ref[slot] = _one_hot_rows_any(ridx_ref, c * ROW_CHUNK, e_lo, e_hi, tr)


def _dispatch_kernel(nrows_ref, used_ref, table_ref, goff_ref, x_ref, ridx_ref, xe_ref,
                     stage, oh, zeros, count, sem, zsem, *, tr, capp):
    bi, ti = pl.program_id(0), pl.program_id(1)
    first = (bi == 0) & (ti == 0)
    last = (bi == pl.num_programs(0) - 1) & (ti == pl.num_programs(1) - 1)
    nch = (nrows_ref[bi, ti] + (ROW_CHUNK - 1)) // ROW_CHUNK
    x = x_ref[0]
    trash = pl.num_programs(0) * N_EXPERTS * capp

    @pl.when(first)
    def _():
        count[0] = 0

    g0 = count[0]

    def piece(slot, p, dst):
        return pltpu.make_async_copy(
            stage.at[slot, pl.ds(p * ROW_ALIGN, ROW_ALIGN), :],
            xe_ref.at[pl.ds(pl.multiple_of(dst, ROW_ALIGN), ROW_ALIGN), :],
            sem.at[slot])

    def wait_slot(slot):
        for p in range(PIECES):
            piece(slot, p, 0).wait()

    span = _build_one_hot(oh, 0, ridx_ref, goff_ref, 0, tr)
    _patch_one_hot(oh, 0, ridx_ref, 0, *span, tr)

    def body(c, carry):
        slot = (g0 + c) % STAGE_SLOTS
        pl.when(g0 + c >= STAGE_SLOTS)(lambda: wait_slot(slot))
        span = _build_one_hot(oh, (c + 1) % 2, ridx_ref, goff_ref, c + 1, tr)
        stage[slot] = _dot(oh[c % 2], x).astype(BF16)
        for p in range(PIECES):
            dst = table_ref[0, 0, c * PIECES + p]
            dst = jnp.where(dst >= 0, dst, trash + (slot * PIECES + p) * ROW_ALIGN)
            piece(slot, p, dst).start()
        _patch_one_hot(oh, (c + 1) % 2, ridx_ref, c + 1, *span, tr)
        return carry

    lax.fori_loop(0, nch, body, 0)
    total = g0 + nch
    count[0] = total

    @pl.when(last)
    def _():
        for k in range(STAGE_SLOTS):
            pl.when(total > k)(lambda k=k: wait_slot((total - 1 - k) % STAGE_SLOTS))

    def zero_piece(row):
        return pltpu.make_async_copy(
            zeros, xe_ref.at[pl.ds(pl.multiple_of(row, ROW_ALIGN), ROW_ALIGN), :], zsem)

    def zero_rows(first, n_pieces):
        def start(k, carry):
            zero_piece(first + k * ROW_ALIGN).start()
            return carry

        def wait(k, carry):
            zero_piece(first + k * ROW_ALIGN).wait()
            return carry

        lax.fori_loop(0, n_pieces, start, 0)
        lax.fori_loop(0, n_pieces, wait, 0)

    @pl.when(ti == pl.num_programs(1) - 1)
    def _():
        zeros[...] = jnp.zeros(zeros.shape, BF16)
        for e in range(N_EXPERTS):
            used = used_ref[bi, e]
            zero_rows((bi * N_EXPERTS + e) * capp + used, (capp - used) // ROW_ALIGN)

        @pl.when(bi == pl.num_programs(0) - 1)
        def _():
            zero_rows(trash, TRASH_ROWS // ROW_ALIGN)


def _dispatch(h_ext, ridx5, table3, goff3, nrows, used, capp, tt):
    b, t, dx = h_ext.shape
    nt = t // tt
    tr = tt // LANES
    pl_lanes = table3.shape[2]
    grid_spec = pltpu.PrefetchScalarGridSpec(
        num_scalar_prefetch=2,
        grid=(b, nt),
        in_specs=[pl.BlockSpec((1, 1, pl_lanes), lambda bi, ti, nr, us: (bi * nt + ti, 0, 0),
                               memory_space=pltpu.SMEM),
                  pl.BlockSpec((1, 1, LANES), lambda bi, ti, nr, us: (bi * nt + ti, 0, 0),
                               memory_space=pltpu.SMEM),
                  pl.BlockSpec((1, tt, dx), lambda bi, ti, nr, us: (bi, ti, 0)),
                  pl.BlockSpec((1, N_EXPERTS, 1, tr, LANES), lambda bi, ti, nr, us: (bi, 0, ti, 0, 0))],
        out_specs=pl.BlockSpec(memory_space=pl.ANY),
        scratch_shapes=[pltpu.VMEM((STAGE_SLOTS, ROW_CHUNK, dx), BF16),
                        pltpu.VMEM((2, ROW_CHUNK, tt), BF16), pltpu.VMEM((ROW_ALIGN, dx), BF16),
                        pltpu.SMEM((1,), I32),
                        pltpu.SemaphoreType.DMA((STAGE_SLOTS,)), pltpu.SemaphoreType.DMA(())],
    )
    return pl.pallas_call(
        functools.partial(_dispatch_kernel, tr=tr, capp=capp),
        grid_spec=grid_spec,
        out_shape=jax.ShapeDtypeStruct((b * N_EXPERTS * capp + TRASH_ROWS, dx), BF16),
        compiler_params=_cparams("arbitrary", "arbitrary"),
    )(nrows, used, table3, goff3, h_ext, ridx5)


def _ffn_kernel(used_ref, x_ref, wg_ref, wu_ref, wd_ref, o_ref, *, rb):
    e, bi = pl.program_id(0), pl.program_id(1)
    capp, d = o_ref.shape[2], o_ref.shape[3]
    nblk = (used_ref[bi, e] + (rb - 1)) // rb
    lane = lax.broadcasted_iota(I32, (1, LANES), 1)
    own = (lane == e) | (lane == e + N_EXPERTS) | (lane == e + 2 * N_EXPERTS)

    def body(j, carry):
        rows = pl.ds(pl.multiple_of(j * rb, rb), rb)
        xb = x_ref[rows, :]
        xm = xb[:, :d]
        val = jnp.sum(jnp.where(own, xb[:, d:].astype(F32), 0.0), axis=1, keepdims=True)
        hid = (_silu(_dot(xm, wg_ref[0, 0])) * _dot(xm, wu_ref[0, 0])).astype(BF16)
        o_ref[0, 0, rows, :] = (_dot(hid, wd_ref[0, 0]) * val).astype(BF16)
        return carry

    def zero(j, carry):
        o_ref[0, 0, pl.ds(pl.multiple_of(j * rb, rb), rb), :] = jnp.zeros((rb, d), BF16)
        return carry

    lax.fori_loop(0, nblk, body, 0)
    lax.fori_loop(nblk, capp // rb, zero, 0)


def _ffn(xe, used, wg, wu, wd, layer, capp, rb):
    b, e_n = used.shape
    dx = xe.shape[1]
    d, f = wg.shape[2], wg.shape[3]
    grid_spec = pltpu.PrefetchScalarGridSpec(
        num_scalar_prefetch=1,
        grid=(e_n, b),
        in_specs=[pl.BlockSpec((capp, dx), lambda e, bi, u: (bi * e_n + e, 0)),
                  pl.BlockSpec((1, 1, d, f), lambda e, bi, u: (layer, e, 0, 0)),
                  pl.BlockSpec((1, 1, d, f), lambda e, bi, u: (layer, e, 0, 0)),
                  pl.BlockSpec((1, 1, f, d), lambda e, bi, u: (layer, e, 0, 0))],
        out_specs=pl.BlockSpec((1, 1, capp, d), lambda e, bi, u: (bi, e, 0, 0)),
    )
    return pl.pallas_call(
        functools.partial(_ffn_kernel, rb=rb),
        grid_spec=grid_spec,
        out_shape=jax.ShapeDtypeStruct((b, e_n, capp, d), BF16),
        compiler_params=_cparams("arbitrary", "arbitrary"),
    )(used, xe, wg, wu, wd)


def _combine_kernel(nrows_ref, table_ref, next_table_ref, goff_ref, ridx_ref, x_ref, g2_ref, fg_ref,
                    ye_ref, o_ref, stage, extra, oh, acc, sem, xsem, *, tr, final):
    bi, ti = pl.program_id(0), pl.program_id(1)
    step = bi * pl.num_programs(1) + ti
    n_steps = pl.num_programs(0) * pl.num_programs(1)
    par = step % 2

    def chunks(s):
        return (nrows_ref[s] + (ROW_CHUNK - 1)) // ROW_CHUNK

    nch = chunks(step)
    n_pre = jnp.minimum(nch, COMBINE_AHEAD)

    def piece(tbl, c, p, dst, dsem):
        src = tbl[0, 0, c * PIECES + p]
        return pltpu.make_async_copy(
            ye_ref.at[pl.ds(pl.multiple_of(jnp.maximum(src, 0), ROW_ALIGN), ROW_ALIGN), :],
            dst.at[pl.ds(p * ROW_ALIGN, ROW_ALIGN), :], dsem)

    def fetch_tile(tbl, parity, n):
        for c in range(COMBINE_AHEAD):
            @pl.when(c < n)
            def _(c=c):
                for p in range(PIECES):
                    piece(tbl, c, p, stage.at[parity, c], sem.at[parity, c]).start()

    pl.when(step == 0)(lambda: fetch_tile(table_ref, 0, n_pre))

    @pl.when(step + 1 < n_steps)
    def _():
        fetch_tile(next_table_ref, 1 - par, jnp.minimum(chunks(step + 1), COMBINE_AHEAD))

    acc[...] = jnp.zeros(acc.shape, F32)
    span = _build_one_hot(oh, 0, ridx_ref, goff_ref, 0, tr)
    _patch_one_hot(oh, 0, ridx_ref, 0, *span, tr)

    def accumulate(c, rows):
        span = _build_one_hot(oh, (c + 1) % 2, ridx_ref, goff_ref, c + 1, tr)
        acc[...] += _dot_tn(oh[c % 2], rows[...])
        _patch_one_hot(oh, (c + 1) % 2, ridx_ref, c + 1, *span, tr)

    def prefetched(c, carry):
        for p in range(PIECES):
            piece(table_ref, c, p, stage.at[par, c], sem.at[par, c]).wait()
        accumulate(c, stage.at[par, c])
        return carry

    def on_the_spot(c, carry):
        for p in range(PIECES):
            piece(table_ref, c, p, extra, xsem).start()
        for p in range(PIECES):
            piece(table_ref, c, p, extra, xsem).wait()
        accumulate(c, extra)
        return carry

    lax.fori_loop(0, n_pre, prefetched, 0)
    lax.fori_loop(n_pre, nch, on_the_spot, 0)
    xo = x_ref[0] + g2_ref[0] * acc[...]
    if final:
        xo = xo * lax.rsqrt(jnp.mean(xo * xo, axis=-1, keepdims=True) + EPS) * fg_ref[...]
    o_ref[0] = xo


def _combine(ye_flat, ridx5, table3, goff3, nrows, x, g2, fg, tt, final):
    b, t, d = x.shape
    nt = t // tt
    tr = tt // LANES
    pl_lanes = table3.shape[2]
    grid_spec = pltpu.PrefetchScalarGridSpec(
        num_scalar_prefetch=1,
        grid=(b, nt),
        in_specs=[pl.BlockSpec((1, 1, pl_lanes), lambda bi, ti, nr: (bi * nt + ti, 0, 0),
                               memory_space=pltpu.SMEM),
                  pl.BlockSpec((1, 1, pl_lanes),
                               lambda bi, ti, nr: (jnp.minimum(bi * nt + ti + 1, b * nt - 1), 0, 0),
                               memory_space=pltpu.SMEM),
                  pl.BlockSpec((1, 1, LANES), lambda bi, ti, nr: (bi * nt + ti, 0, 0),
                               memory_space=pltpu.SMEM),
                  pl.BlockSpec((1, N_EXPERTS, 1, tr, LANES), lambda bi, ti, nr: (bi, 0, ti, 0, 0)),
                  pl.BlockSpec((1, tt, d), lambda bi, ti, nr: (bi, ti, 0)),
                  pl.BlockSpec((1, 1, d), lambda bi, ti, nr: (bi, 0, 0)),
                  pl.BlockSpec((1, d), lambda bi, ti, nr: (0, 0)),
                  pl.BlockSpec(memory_space=pl.ANY)],
        out_specs=pl.BlockSpec((1, tt, d), lambda bi, ti, nr: (bi, ti, 0)),
        scratch_shapes=[pltpu.VMEM((2, COMBINE_AHEAD, ROW_CHUNK, d), BF16),
                        pltpu.VMEM((ROW_CHUNK, d), BF16), pltpu.VMEM((2, ROW_CHUNK, tt), BF16),
                        pltpu.VMEM((tt, d), F32),
                        pltpu.SemaphoreType.DMA((2, COMBINE_AHEAD)), pltpu.SemaphoreType.DMA(())],
    )
    return pl.pallas_call(
        functools.partial(_combine_kernel, tr=tr, final=final),
        grid_spec=grid_spec,
        out_shape=jax.ShapeDtypeStruct((b, t, d), F32),
        compiler_params=_cparams("arbitrary", "arbitrary"),
    )(nrows.reshape(b * nt), table3, table3, goff3, ridx5, x, g2, fg, ye_flat)


def _moe_tiles(t):
    tt = min(512, t)
    t_sel = max(t, 8 * LANES)
    return tt, t_sel


def _ec_moe(x, g, shift, scale, gate, wr_pad, wg, wu, wd, layer, final_g, final):
    b, t, d = x.shape
    e_n = N_EXPERTS
    cap = EC_CAPACITY_FACTOR * t // e_n
    tt, t_sel = _moe_tiles(t)
    nt, tr = t // tt, tt // LANES
    capp = cap + ROW_ALIGN * nt
    rb = min(ROW_CHUNK, capp)
    capp = -(-capp // rb) * rb
    pl_lanes = -(-(tt + ROW_ALIGN) // LANES) * LANES

    h_ext, aff_t = _router(x, g, shift, scale, wr_pad, min(512, t))
    if t_sel > t:
        aff_t = jnp.pad(aff_t, ((0, 0), (0, 0), (0, t_sel - t)), constant_values=-1.0)
    aff_r = aff_t.reshape(b, e_n, t_sel // LANES, LANES)
    ridx, table, goff, used = _select(aff_r, cap, tr, capp, pl_lanes)
    ridx5 = ridx[:, :, :nt * tr].reshape(b, e_n, nt, tr, LANES)
    table3 = table[:, :nt * tr:tr].reshape(b * nt, 1, pl_lanes)
    goff3 = goff[:, :nt * tr:tr].reshape(b * nt, 1, LANES)
    nrows2 = goff[:, :nt * tr:tr, e_n]
    used2 = used[:, :, 0, 0]

    xe = _dispatch(h_ext, ridx5, table3, goff3, nrows2, used2, capp, tt)
    ye = _ffn(xe, used2, wg, wu, wd, layer, capp, rb)
    return _combine(ye.reshape(b * e_n * capp, d), ridx5, table3, goff3, nrows2, x, gate, final_g, tt,
                    final)


def kernel(x, c, ctx, c_ctx, ada_w, ada_b, norm_g, gla_w_in, gla_w_a2, gla_b_a2, gla_norm_g,
           gla_w_out, conv_w_in, conv_k, conv_w_out, router_w, expert_w_gate, expert_w_up,
           expert_w_down, final_norm_g):
    b, t, d = x.shape
    depth = ada_w.shape[0]
    assert depth == 2 and b + 1 <= 8
    dk = gla_w_a2.shape[3]
    dv = gla_w_out.shape[1]
    hk, hv = dk // GLA_HEADS, dv // GLA_HEADS

    cvec = jnp.zeros((8, d), F32).at[:b].set(c).at[b].set(c_ctx)
    mods = _ada(cvec, ada_w, ada_b)

    def mod_vectors(layer, rows):
        m = mods[layer, rows].reshape(rows.shape[0], 1, 6, d)
        return [m[:, :, i, :] for i in range(6)]

    rows_x = jnp.arange(b)
    rows_c = jnp.full((b,), b)
    wr_pad = [jnp.pad(router_w[i], ((0, 0), (0, LANES - N_EXPERTS))) for i in range(depth)]
    fg = final_norm_g.reshape(1, d)

    sh1x, sc1x, g1x, sh2x, sc2x, g2x = mod_vectors(0, rows_x)
    sh1c, sc1c, g1c, sh2c, sc2c, g2c = mod_vectors(0, rows_c)
    n_main = 2 * dk + 2 * dv
    w_main = jnp.pad(gla_w_in[0], ((0, 0), (0, LANES - 2 * GLA_RANK))).astype(BF16)
    assert w_main.shape[1] == n_main + LANES
    w2 = jnp.zeros((LANES, 2 * dk), F32)
    w2 = w2.at[:GLA_RANK, :dk].set(gla_w_a2[0, 0]).at[GLA_RANK:2 * GLA_RANK, dk:].set(gla_w_a2[0, 1])
    b2 = gla_b_a2[0].reshape(1, 2 * dk)
    gn = gla_norm_g[0].reshape(1, hv)
    w_out = gla_w_out[0].astype(BF16)
    g_n1 = norm_g[0, 0].reshape(1, d)
    g_n2 = norm_g[0, 1].reshape(1, d)

    tc = ctx.shape[1]
    qc, kc, vc, rc, lac = _gla_in(ctx, g_n1, sh1c, sc1c, w_main, w2, b2, min(512, tc))
    s0 = jnp.zeros((b, GLA_HEADS, hv, hk), F32)
    ocf, ocb, s_cf, s_cb = _gla_scan(qc, kc, vc, lac, s0, s0)
    qx, kx, vx, rx, lax_ = _gla_in(x, g_n1, sh1x, sc1x, w_main, w2, b2, 512)
    oxf, oxb, _, _, wg, wu, wd = _gla_scan(qx, kx, vx, lax_, s_cf, s_cb,
                                           cast=(expert_w_gate, expert_w_up, expert_w_down))
    wg, wu, wd = (w.reshape(s.shape) for w, s in
                  zip((wg, wu, wd), (expert_w_gate, expert_w_up, expert_w_down)))
    x = _gla_out(oxf, oxb, rx, x, gn, w_out, g1x, 512)
    x = _ec_moe(x, g_n2, sh2x, sc2x, g2x, wr_pad[0], wg, wu, wd, 0, fg, False)

    ctx = _gla_out(ocf, ocb, rc, ctx, gn, w_out, g1c, min(512, tc))
    ctx = _ec_moe(ctx, g_n2, sh2c, sc2c, g2c, wr_pad[0], wg, wu, wd, 0, fg, False)
    del ctx

    sh1x, sc1x, g1x, sh2x, sc2x, g2x = mod_vectors(1, rows_x)
    x = _conv_mix(x, norm_g[1, 0].reshape(1, d), sh1x, sc1x, conv_w_in[0].astype(BF16), conv_k[0],
                  conv_w_out[0].astype(BF16), g1x, GRID_W, 512)
    return _ec_moe(x, norm_g[1, 1].reshape(1, d), sh2x, sc2x, g2x, wr_pad[1], wg, wu, wd, 1, fg, True)
```

```python
import functools

import jax
import jax.numpy as jnp
from jax import lax
from jax.experimental import pallas as pl
from jax.experimental.pallas import tpu as pltpu

F32 = jnp.float32
BF16 = jnp.bfloat16
I32 = jnp.int32
HIGHEST = lax.Precision.HIGHEST

EPS = 1e-6
GRID_W = 64
GLA_HEADS = 4
GLA_RANK = 16
GLA_TAU = 16.0
N_EXPERTS = 16
EC_CAPACITY_FACTOR = 2
CONV_WIDTH = 3

LANES = 128
ROW_ALIGN = 16
GLA_CHUNK = 256
GLA_DIAG = 8
ROW_CHUNK = 256
PIECES = ROW_CHUNK // ROW_ALIGN
ONE_HOT_SLOTS = 3
DISPATCH_CHUNKS = 5
TRASH_ROWS = (2 * DISPATCH_CHUNKS + 1) * ROW_CHUNK
COMBINE_AHEAD = 6
MIN_NORMAL_BITS = 0x00800000
REFINE_STEPS = 40
VMEM_LIMIT = 60 * 1024 * 1024


def _cparams(*sem):
    return pltpu.CompilerParams(dimension_semantics=sem, vmem_limit_bytes=VMEM_LIMIT)


def _dot(a, b):
    return jnp.dot(a, b, preferred_element_type=F32)


def _dot_nt(a, b):
    return lax.dot_general(a, b, (((1,), (1,)), ((), ())), preferred_element_type=F32)


def _dot_tn(a, b):
    return lax.dot_general(a, b, (((0,), (0,)), ((), ())), preferred_element_type=F32)


def _xdot(a, b):
    return jnp.dot(a, b, precision=HIGHEST, preferred_element_type=F32)


def _split2(x):
    hi = x.astype(BF16)
    return hi, (x - hi.astype(F32)).astype(BF16)


def _silu(x):
    return x * jax.nn.sigmoid(x)


def _norm_mod(x, g, shift, scale):
    y = x * lax.rsqrt(jnp.mean(x * x, axis=-1, keepdims=True) + EPS) * g
    return y * (1.0 + scale) + shift


def _split3(x):
    hi = x.astype(BF16)
    r1 = x - hi.astype(F32)
    mid = r1.astype(BF16)
    lo = (r1 - mid.astype(F32)).astype(BF16)
    return hi, mid, lo


def _ada_kernel(c_ref, w_ref, b_ref, o_ref):
    o_ref[0] = _xdot(_silu(c_ref[...]), w_ref[0]) + b_ref[0]


def _ada(cvec, ada_w, ada_b):
    depth, d, n6 = ada_w.shape
    nt = n6 // d
    return pl.pallas_call(
        _ada_kernel,
        grid=(depth, nt),
        in_specs=[pl.BlockSpec((8, d), lambda l, n: (0, 0)),
                  pl.BlockSpec((1, d, d), lambda l, n: (l, 0, n)),
                  pl.BlockSpec((1, 1, d), lambda l, n: (l, 0, n))],
        out_specs=pl.BlockSpec((1, 8, d), lambda l, n: (l, 0, n)),
        out_shape=jax.ShapeDtypeStruct((depth, 8, n6), F32),
        compiler_params=_cparams("arbitrary", "arbitrary"),
    )(cvec, ada_w, ada_b.reshape(depth, 1, n6))


def _gla_in_kernel(x_ref, g_ref, sh_ref, sc_ref, w_ref, w2_ref, b2_ref,
                   q_ref, k_ref, v_ref, r_ref, la_ref, *, dk, dv):
    h = _norm_mod(x_ref[0], g_ref[...], sh_ref[0], sc_ref[0])
    y = _dot(h.astype(BF16), w_ref[...])
    hk = dk // GLA_HEADS
    q_ref[0] = y[:, :dk] * (hk ** -0.5)
    k_ref[0] = y[:, dk:2 * dk]
    v_ref[0] = y[:, 2 * dk:2 * dk + dv]
    r_ref[0] = y[:, 2 * dk + dv:2 * dk + 2 * dv].astype(BF16)
    a_hi, a_lo = _split2(y[:, 2 * dk + 2 * dv:])
    a3 = (a_hi.astype(F32) + pltpu.roll(a_lo.astype(F32), 2 * GLA_RANK, 1)
          + pltpu.roll(a_hi.astype(F32), 4 * GLA_RANK, 1))
    z = _dot(a3.astype(BF16), w2_ref[...]) + b2_ref[...]
    log_sig = jnp.minimum(z, 0.0) - jnp.log(1.0 + jnp.exp(-jnp.abs(z)))
    la_ref[0] = log_sig * (1.0 / GLA_TAU)


def _gla_in(x, g, shift, scale, w_main, w2, b2, tm):
    b, t, d = x.shape
    dk = w2.shape[1] // 2
    dv = (w_main.shape[1] - LANES - 2 * dk) // 2
    nw = w_main.shape[1]
    tok = lambda width: pl.BlockSpec((1, tm, width), lambda bi, ti: (bi, ti, 0))
    vec = pl.BlockSpec((1, 1, d), lambda bi, ti: (bi, 0, 0))
    full = lambda shape: pl.BlockSpec(shape, lambda bi, ti: tuple(0 for _ in shape))
    return pl.pallas_call(
        functools.partial(_gla_in_kernel, dk=dk, dv=dv),
        grid=(b, t // tm),
        in_specs=[tok(d), full((1, d)), vec, vec, full((d, nw)), full((LANES, 2 * dk)), full((1, 2 * dk))],
        out_specs=[tok(dk), tok(dk), tok(dv), tok(dv), tok(2 * dk)],
        out_shape=[jax.ShapeDtypeStruct((b, t, dk), F32), jax.ShapeDtypeStruct((b, t, dk), F32),
                   jax.ShapeDtypeStruct((b, t, dv), F32), jax.ShapeDtypeStruct((b, t, dv), BF16),
                   jax.ShapeDtypeStruct((b, t, 2 * dk), F32)],
        compiler_params=_cparams("arbitrary", "arbitrary"),
    )(x, g, shift, scale, w_main, w2, b2)


def _gla_chunk(q, k, v, la, st, rev):
    c, hk = q.shape
    ii = lax.broadcasted_iota(I32, (c, c), 0)
    jj = lax.broadcasted_iota(I32, (c, c), 1)
    tri = ((jj >= ii) if rev else (jj <= ii)).astype(BF16)
    hi, mid, lo = _split3(la)
    acum = _dot(tri, hi) + _dot(tri, mid) + _dot(tri, lo)
    a_last = acum[0:1] if rev else acum[c - 1:c]

    o = _dot_nt((q * jnp.exp(acum)).astype(BF16), st.astype(BF16))

    xor = ii ^ jj
    rowi = lax.broadcasted_iota(I32, (c, hk), 0)
    p = jnp.zeros((c, c), F32)
    blk = c
    while blk >= 2 * GLA_DIAG:
        half = blk // 2
        a3 = acum.reshape(c // blk, blk, hk)
        ridx = half if rev else half - 1
        ref = jnp.broadcast_to(a3[:, ridx:ridx + 1, :], a3.shape).reshape(c, hk)
        pos = rowi & (blk - 1)
        late = (pos < half) if rev else (pos >= half)
        qe = jnp.where(late, q * jnp.exp(acum - ref), 0.0).astype(BF16)
        ke = jnp.where(late, 0.0, k * jnp.exp(ref - acum)).astype(BF16)
        raw = _dot_nt(qe, ke)
        p = p + (raw if blk == c else jnp.where(xor < blk, raw, 0.0))
        blk = half

    nb = c // GLA_DIAG
    q8 = q.reshape(nb, GLA_DIAG, hk)
    k8 = k.reshape(nb, GLA_DIAG, hk)
    a8 = acum.reshape(nb, GLA_DIAG, hk)
    sub = lax.broadcasted_iota(I32, (nb, GLA_DIAG, hk), 1)
    colmod = jj & (GLA_DIAG - 1)
    pd = jnp.zeros((c, c), F32)
    for j in range(GLA_DIAG):
        kj = jnp.broadcast_to(k8[:, j:j + 1, :], k8.shape)
        aj = jnp.broadcast_to(a8[:, j:j + 1, :], a8.shape)
        valid = (sub <= j) if rev else (sub >= j)
        dec = jnp.where(valid, jnp.exp(a8 - aj), 0.0)
        cj = jnp.sum(q8 * kj * dec, axis=2, keepdims=True).reshape(c, 1)
        pd = jnp.where(colmod == j, cj, pd)
    p = p + jnp.where(xor < GLA_DIAG, pd, 0.0)

    vb = v.astype(BF16)
    o = o + _dot(p.astype(BF16), vb)
    kd = (k * jnp.exp(a_last - acum)).astype(BF16)
    st_new = st * jnp.exp(a_last) + _dot_tn(vb, kd)
    return o, st_new


def _gla_scan_kernel(*refs, n_cast):
    (qf_ref, kf_ref, vf_ref, laf_ref, qb_ref, kb_ref, vb_ref, lab_ref, s0f_ref, s0b_ref) = refs[:10]
    w_in = refs[10:10 + n_cast]
    of_ref, ob_ref, sf_ref, sb_ref = refs[10 + n_cast:14 + n_cast]
    w_out = refs[14 + n_cast:14 + 2 * n_cast]
    stf, stb = refs[14 + 2 * n_cast:]
    n = pl.program_id(2)
    for wi, wo in zip(w_in, w_out):
        wo[0] = wi[0, 0].astype(BF16)

    @pl.when(n == 0)
    def _():
        stf[...] = s0f_ref[0, 0]
        stb[...] = s0b_ref[0, 0]

    o, s = _gla_chunk(qf_ref[0], kf_ref[0], vf_ref[0], laf_ref[0], stf[...], False)
    of_ref[0] = o.astype(BF16)
    stf[...] = s
    o, s = _gla_chunk(qb_ref[0], kb_ref[0], vb_ref[0], lab_ref[0], stb[...], True)
    ob_ref[0] = o.astype(BF16)
    stb[...] = s

    @pl.when(n == pl.num_programs(2) - 1)
    def _():
        sf_ref[0, 0] = stf[...]
        sb_ref[0, 0] = stb[...]


def _gla_scan(q, k, v, la, s0f, s0b, cast=()):
    b, t, dk = q.shape
    dv = v.shape[2]
    hk, hv = dk // GLA_HEADS, dv // GLA_HEADS
    c = GLA_CHUNK
    nc = t // c
    steps = b * GLA_HEADS * nc
    fwd = lambda w, off: pl.BlockSpec((1, c, w), lambda bi, h, n: (bi, n, h + off))
    bwd = lambda w, off: pl.BlockSpec((1, c, w), lambda bi, h, n: (bi, nc - 1 - n, h + off))
    st = pl.BlockSpec((1, 1, hv, hk), lambda bi, h, n: (bi, h, 0, 0))
    cast_in, cast_out, cast_shape = [], [], []
    for w in cast:
        n_l, n_e, rows, cols = w.shape
        per = steps // (n_l * n_e)
        rb = rows // per
        assert per * n_l * n_e == steps and rb * per == rows and rb % ROW_ALIGN == 0
        step = lambda bi, h, n: (bi * GLA_HEADS + h) * nc + n
        cast_in.append(pl.BlockSpec(
            (1, 1, rb, cols),
            lambda bi, h, n, per=per, n_e=n_e: (step(bi, h, n) // per // n_e, step(bi, h, n) // per % n_e,
                                                step(bi, h, n) % per, 0)))
        cast_out.append(pl.BlockSpec(
            (1, rb, cols), lambda bi, h, n, per=per: (step(bi, h, n) // per, step(bi, h, n) % per, 0)))
        cast_shape.append(jax.ShapeDtypeStruct((n_l * n_e, rows, cols), BF16))
    return pl.pallas_call(
        functools.partial(_gla_scan_kernel, n_cast=len(cast)),
        grid=(b, GLA_HEADS, nc),
        in_specs=[fwd(hk, 0), fwd(hk, 0), fwd(hv, 0), fwd(hk, 0),
                  bwd(hk, 0), bwd(hk, 0), bwd(hv, 0), bwd(hk, GLA_HEADS), st, st] + cast_in,
        out_specs=[fwd(hv, 0), bwd(hv, 0), st, st] + cast_out,
        out_shape=[jax.ShapeDtypeStruct((b, t, dv), BF16), jax.ShapeDtypeStruct((b, t, dv), BF16),
                   jax.ShapeDtypeStruct((b, GLA_HEADS, hv, hk), F32),
                   jax.ShapeDtypeStruct((b, GLA_HEADS, hv, hk), F32)] + cast_shape,
        scratch_shapes=[pltpu.VMEM((hv, hk), F32), pltpu.VMEM((hv, hk), F32)],
        compiler_params=_cparams("arbitrary", "arbitrary", "arbitrary"),
    )(q, k, v, la, q, k, v, la, s0f, s0b, *cast)


def _gla_out_kernel(of_ref, ob_ref, r_ref, x_ref, gn_ref, w_ref, g1_ref, o_ref):
    o = of_ref[0].astype(F32) + ob_ref[0].astype(F32)
    hv = gn_ref.shape[1]
    parts = []
    for h in range(GLA_HEADS):
        oh = o[:, h * hv:(h + 1) * hv]
        ms = jnp.mean(oh * oh, axis=-1, keepdims=True)
        parts.append(oh * lax.rsqrt(ms + EPS) * gn_ref[...])
    y = jnp.concatenate(parts, axis=1) * _silu(r_ref[0].astype(F32))
    o_ref[0] = x_ref[0] + g1_ref[0] * _dot(y.astype(BF16), w_ref[...])


def _gla_out(o_f, o_b, r, x, gn, w_out, g1, tm):
    b, t, d = x.shape
    dv = o_f.shape[2]
    tok = lambda width: pl.BlockSpec((1, tm, width), lambda bi, ti: (bi, ti, 0))
    return pl.pallas_call(
        _gla_out_kernel,
        grid=(b, t // tm),
        in_specs=[tok(dv), tok(dv), tok(dv), tok(d),
                  pl.BlockSpec((1, dv // GLA_HEADS), lambda bi, ti: (0, 0)),
                  pl.BlockSpec((dv, d), lambda bi, ti: (0, 0)),
                  pl.BlockSpec((1, 1, d), lambda bi, ti: (bi, 0, 0))],
        out_specs=tok(d),
        out_shape=jax.ShapeDtypeStruct((b, t, d), F32),
        compiler_params=_cparams("arbitrary", "arbitrary"),
    )(o_f, o_b, r, x, gn, w_out, g1)


def _conv_mix_kernel(x_ref, g_ref, sh_ref, sc_ref, win_ref, ck_ref, wout_ref, g1_ref, o_ref, *, seg):
    x = x_ref[0]
    tm, d = x.shape
    h = _norm_mod(x, g_ref[...], sh_ref[0], sc_ref[0])
    y = _dot(h.astype(BF16), win_ref[...])
    bg, cg, v = y[:, :d], y[:, d:2 * d], y[:, 2 * d:]
    u = cg * v
    pos = lax.broadcasted_iota(I32, (tm, 1), 0) % seg
    u_prev = jnp.where(pos == 0, 0.0, pltpu.roll(u, 1, 0))
    u_next = jnp.where(pos == seg - 1, 0.0, pltpu.roll(u, tm - 1, 0))
    conv = u_prev * ck_ref[0:1, :] + u * ck_ref[1:2, :] + u_next * ck_ref[2:3, :]
    o_ref[0] = x + g1_ref[0] * _dot((bg * conv).astype(BF16), wout_ref[...])


def _conv_mix(x, g, shift, scale, w_in, ck, w_out, g1, seg, tm):
    b, t, d = x.shape
    tok = pl.BlockSpec((1, tm, d), lambda bi, ti: (bi, ti, 0))
    vec = pl.BlockSpec((1, 1, d), lambda bi, ti: (bi, 0, 0))
    full = lambda shape: pl.BlockSpec(shape, lambda bi, ti: tuple(0 for _ in shape))
    return pl.pallas_call(
        functools.partial(_conv_mix_kernel, seg=seg),
        grid=(b, t // tm),
        in_specs=[tok, full((1, d)), vec, vec, full((d, 3 * d)), full((CONV_WIDTH, d)), full((d, d)), vec],
        out_specs=tok,
        out_shape=jax.ShapeDtypeStruct((b, t, d), F32),
        compiler_params=_cparams("arbitrary", "arbitrary"),
    )(x, g, shift, scale, w_in, ck, w_out, g1)


def _router_kernel(x_ref, g_ref, sh_ref, sc_ref, wr_ref, h_ref, aff_ref):
    d = x_ref.shape[2]
    h = _norm_mod(x_ref[0], g_ref[...], sh_ref[0], sc_ref[0])
    h_hi, h_lo = _split2(h)
    p = _dot(h_hi, wr_ref[...]) + _dot(h_lo, wr_ref[...])
    logits = p + pltpu.roll(p, LANES - N_EXPERTS, 1)
    lane = lax.broadcasted_iota(I32, logits.shape, 1)
    is_e = lane < N_EXPERTS
    m = jnp.max(jnp.where(is_e, logits, -jnp.inf), axis=-1, keepdims=True)
    ex = jnp.where(is_e, jnp.exp(logits - m), 0.0)
    aff = ex / jnp.sum(ex, axis=-1, keepdims=True)
    aff_ref[0] = aff.T[:N_EXPERTS, :]
    hi, mid, lo = _split3(aff)
    ext = (hi.astype(F32) + pltpu.roll(mid.astype(F32), N_EXPERTS, 1)
           + pltpu.roll(lo.astype(F32), 2 * N_EXPERTS, 1))
    h_ref[0, :, :d] = h.astype(BF16)
    h_ref[0, :, d:] = ext.astype(BF16)


def _router(x, g, shift, scale, wr_pad, tm):
    b, t, d = x.shape
    tok = lambda width: pl.BlockSpec((1, tm, width), lambda bi, ti: (bi, ti, 0))
    vec = pl.BlockSpec((1, 1, d), lambda bi, ti: (bi, 0, 0))
    return pl.pallas_call(
        _router_kernel,
        grid=(b, t // tm),
        in_specs=[tok(d), pl.BlockSpec((1, d), lambda bi, ti: (0, 0)), vec, vec,
                  pl.BlockSpec((d, LANES), lambda bi, ti: (0, 0))],
        out_specs=[tok(d + LANES), pl.BlockSpec((1, N_EXPERTS, tm), lambda bi, ti: (bi, 0, ti))],
        out_shape=[jax.ShapeDtypeStruct((b, t, d + LANES), BF16),
                   jax.ShapeDtypeStruct((b, N_EXPERTS, t), F32)],
        compiler_params=_cparams("arbitrary", "arbitrary"),
    )(x, g, shift, scale, wr_pad)


def _select_kernel(aff_ref, ridx_ref, table_ref, goff_ref, used_ref, *, cap, tr, capp, pl_lanes):
    e_n = N_EXPERTS
    a = aff_ref[0]
    r_n = a.shape[1]

    def count_ge(v):
        return jnp.sum(jnp.where(a >= v, 1.0, 0.0), axis=(1, 2), keepdims=True)

    def search(it, cur):
        cand = cur | (jnp.int32(1) << (30 - it))
        return jnp.where(count_ge(lax.bitcast_convert_type(cand, F32)) >= cap, cand, cur)

    thr = lax.fori_loop(0, 31, search, jnp.zeros((e_n, 1, 1), I32))
    normal = thr >= MIN_NORMAL_BITS
    lo = lax.bitcast_convert_type(jnp.where(normal, thr, 0), F32)
    hi = lax.bitcast_convert_type(jnp.where(normal, thr + 1, MIN_NORMAL_BITS), F32)

    def refine(it, lh):
        lo_, hi_ = lh
        mid = 0.5 * (lo_ + hi_)
        ok = count_ge(mid) >= cap
        return jnp.where(ok, mid, lo_), jnp.where(ok, hi_, mid)

    lo, hi = lax.fori_loop(0, REFINE_STEPS, refine, (lo, hi))
    gt = a >= hi
    eq = (a >= lo) & jnp.logical_not(gt)
    need = cap - jnp.sum(jnp.where(gt, 1.0, 0.0), axis=(1, 2), keepdims=True)

    li = lax.broadcasted_iota(I32, (LANES, LANES), 0)
    lj = lax.broadcasted_iota(I32, (LANES, LANES), 1)
    upper = (li <= lj).astype(BF16)
    ones = jnp.ones((LANES, LANES), BF16)
    ri = lax.broadcasted_iota(I32, (r_n, r_n), 0)
    rj = lax.broadcasted_iota(I32, (r_n, r_n), 1)
    strict_lower = (rj < ri).astype(BF16)

    def cumsum_tokens(mask):
        m = jnp.where(mask, 1.0, 0.0).astype(BF16).reshape(e_n * r_n, LANES)
        within = _dot(m, upper).reshape(e_n, r_n, LANES)
        rs = _dot(m, ones).reshape(e_n, r_n, LANES)
        pre = jnp.stack([_dot(strict_lower, rs[e].astype(BF16)) for e in range(e_n)])
        return within + pre, rs

    eq_rank, _ = cumsum_tokens(eq)
    sel = gt | (eq & (eq_rank <= need))
    incl, rs = cumsum_tokens(sel)
    pos = incl - jnp.where(sel, 1.0, 0.0)

    same_tile = ((ri // tr) == (rj // tr)).astype(F32)
    prev_tile = ((rj // tr) < (ri // tr)).astype(F32)
    first_row = (rj % tr) == 0
    prev_first = jnp.where(first_row, prev_tile, 0.0)
    all_first = jnp.where(first_row, 1.0, 0.0)

    goff = jnp.zeros((r_n, LANES), F32)
    lane = lax.broadcasted_iota(I32, (r_n, LANES), 1)
    gtab = jnp.zeros((r_n, LANES), F32)
    rows = []
    tbl = jnp.full((r_n, pl_lanes), -1.0, F32)
    row16 = (lax.broadcasted_iota(I32, (r_n, pl_lanes), 1) * ROW_ALIGN).astype(F32)
    rep = lambda x: jnp.concatenate([x] * (pl_lanes // LANES), axis=1)
    base = (pl.program_id(0) * e_n * capp).astype(F32)
    for e in range(e_n):
        n_tile = _xdot(same_tile, rs[e])
        pos_start = _xdot(prev_tile, rs[e])
        n_pad = jnp.floor((n_tile + (ROW_ALIGN - 1)) * (1.0 / ROW_ALIGN)) * ROW_ALIGN
        off = _xdot(prev_first, n_pad)
        used_ref[0, e] = _xdot(all_first, n_pad).astype(I32)
        rows.append(jnp.where(sel[e], goff + pos[e] - pos_start, -1.0))
        gtab = jnp.where(lane == e, goff, gtab)
        g0, g1 = rep(goff), rep(goff + n_pad)
        inside = (row16 >= g0) & (row16 < g1)
        tbl = jnp.where(inside, base + e * capp + rep(off) + row16 - g0, tbl)
        goff = goff + n_pad
    ridx_ref[0] = jnp.stack(rows).astype(I32)
    table_ref[0] = tbl.astype(I32)
    goff_ref[0] = jnp.where(lane >= e_n, goff, gtab).astype(I32)


def _select(aff_r, cap, tr, capp, pl_lanes):
    b, e_n, r_n, _ = aff_r.shape
    blk4 = pl.BlockSpec((1, e_n, r_n, LANES), lambda bi: (bi, 0, 0, 0))
    return pl.pallas_call(
        functools.partial(_select_kernel, cap=cap, tr=tr, capp=capp, pl_lanes=pl_lanes),
        grid=(b,),
        in_specs=[blk4],
        out_specs=[blk4, pl.BlockSpec((1, r_n, pl_lanes), lambda bi: (bi, 0, 0)),
                   pl.BlockSpec((1, r_n, LANES), lambda bi: (bi, 0, 0)), blk4],
        out_shape=[jax.ShapeDtypeStruct((b, e_n, r_n, LANES), I32),
                   jax.ShapeDtypeStruct((b, r_n, pl_lanes), I32),
                   jax.ShapeDtypeStruct((b, r_n, LANES), I32),
                   jax.ShapeDtypeStruct((b, e_n, r_n, LANES), I32)],
        compiler_params=_cparams("arbitrary"),
    )(aff_r)


def _expert_span(goff_ref, base):
    e_lo = jnp.int32(0)
    e_hi = jnp.int32(0)
    for e in range(N_EXPERTS):
        e_lo += (goff_ref[0, 0, e + 1] <= base).astype(I32)
        e_hi += (goff_ref[0, 0, e] < base + ROW_CHUNK).astype(I32)
    return e_lo, e_hi


def _one_hot_rows(ridx_ref, base, e_lo, e_hi, tr):
    riota = lax.broadcasted_iota(I32, (ROW_CHUNK, LANES), 0) + base
    pieces = []
    for q in range(tr):
        acc = jnp.zeros((ROW_CHUNK, LANES), F32)
        for k in range(ONE_HOT_SLOTS):
            e = jnp.minimum(e_lo + k, N_EXPERTS - 1)
            row = jnp.where(e_lo + k < e_hi, ridx_ref[0, e, 0, pl.ds(q, 1), :], -2)
            acc = jnp.where(row == riota, 1.0, acc)
        pieces.append(acc.astype(BF16))
    return jnp.concatenate(pieces, axis=1)


def _one_hot_rows_any(ridx_ref, base, e_lo, e_hi, tr):
    riota = lax.broadcasted_iota(I32, (ROW_CHUNK, LANES), 0) + base
    pieces = []
    for q in range(tr):
        def add_expert(e, acc, q=q):
            return jnp.where(ridx_ref[0, e, 0, pl.ds(q, 1), :] == riota, 1.0, acc)
        acc = lax.fori_loop(e_lo, e_hi, add_expert, jnp.zeros((ROW_CHUNK, LANES), F32))
        pieces.append(acc.astype(BF16))
    return jnp.concatenate(pieces, axis=1)


def _build_one_hot(oh_ref, slot, ridx_ref, goff_ref, c, tr):
    base = c * ROW_CHUNK
    e_lo, e_hi = _expert_span(goff_ref, base)
    oh_ref[slot] = _one_hot_rows(ridx_ref, base, e_lo, e_hi, tr)
    return e_lo, e_hi


def _patch_one_hot(oh_ref, slot, ridx_ref, c, e_lo, e_hi, tr):
    @pl.when(e_hi - e_lo > ONE_HOT_SLOTS)
    def _():
        oh_ref[slot] = _one_hot_rows_any(ridx_ref, c * ROW_CHUNK, e_lo, e_hi, tr)


def _dispatch_kernel(nrows_ref, used_ref, table_ref, goff_ref, next_goff_ref, x_ref, ridx_ref,
                     next_ridx_ref, xe_ref, stage, extra, oh, oh_extra, zeros, sem, xsem, zsem,
                     *, tr, capp):
    bi, ti = pl.program_id(0), pl.program_id(1)
    step = bi * pl.num_programs(1) + ti
    n_steps = pl.num_programs(0) * pl.num_programs(1)
    par = step % 2
    nch = (nrows_ref[bi, ti] + (ROW_CHUNK - 1)) // ROW_CHUNK
    x = x_ref[0]
    trash = pl.num_programs(0) * N_EXPERTS * capp
    n_big = DISPATCH_CHUNKS * PIECES

    def piece(src, j, dst, dsem):
        return pltpu.make_async_copy(
            src.at[pl.ds(j * ROW_ALIGN, ROW_ALIGN), :],
            xe_ref.at[pl.ds(pl.multiple_of(dst, ROW_ALIGN), ROW_ALIGN), :], dsem)

    def wait_big(parity):
        for j in range(n_big):
            piece(stage.at[parity], j, 0, sem.at[parity]).wait()

    def build_tile(rref, gref, parity):
        return [_build_one_hot(oh.at[parity], c, rref, gref, c, tr) for c in range(DISPATCH_CHUNKS)]

    def patch_tile(rref, parity, spans):
        for c, span in enumerate(spans):
            _patch_one_hot(oh.at[parity], c, rref, c, *span, tr)

    pl.when(step == 0)(lambda: patch_tile(ridx_ref, 0, build_tile(ridx_ref, goff_ref, 0)))
    pl.when(step >= 2)(lambda: wait_big(par))

    next_spans = build_tile(next_ridx_ref, next_goff_ref, 1 - par)
    rows = DISPATCH_CHUNKS * ROW_CHUNK
    stage[par] = _dot(oh[par].reshape(rows, oh.shape[3]), x).astype(BF16)
    for j in range(n_big):
        dst = table_ref[0, 0, j]
        dst = jnp.where(dst >= 0, dst, trash + (par * n_big + j) * ROW_ALIGN)
        piece(stage.at[par], j, dst, sem.at[par]).start()
    patch_tile(next_ridx_ref, 1 - par, next_spans)

    def one_more(c, carry):
        span = _build_one_hot(oh_extra, 0, ridx_ref, goff_ref, c, tr)
        _patch_one_hot(oh_extra, 0, ridx_ref, c, *span, tr)
        extra[...] = _dot(oh_extra[0], x).astype(BF16)
        for p in range(PIECES):
            dst = table_ref[0, 0, c * PIECES + p]
            dst = jnp.where(dst >= 0, dst, trash + (2 * n_big + p) * ROW_ALIGN)
            piece(extra, p, dst, xsem).start()
        for p in range(PIECES):
            piece(extra, p, 0, xsem).wait()
        return carry

    lax.fori_loop(DISPATCH_CHUNKS, nch, one_more, 0)

    @pl.when(step == n_steps - 1)
    def _():
        wait_big(par)
        pl.when(n_steps >= 2)(lambda: wait_big(1 - par))

    def zero_piece(row):
        return pltpu.make_async_copy(
            zeros, xe_ref.at[pl.ds(pl.multiple_of(row, ROW_ALIGN), ROW_ALIGN), :], zsem)

    def zero_rows(first, n_pieces):
        def start(k, carry):
            zero_piece(first + k * ROW_ALIGN).start()
            return carry

        def wait(k, carry):
            zero_piece(first + k * ROW_ALIGN).wait()
            return carry

        lax.fori_loop(0, n_pieces, start, 0)
        lax.fori_loop(0, n_pieces, wait, 0)

    @pl.when(ti == pl.num_programs(1) - 1)
    def _():
        zeros[...] = jnp.zeros(zeros.shape, BF16)
        for e in range(N_EXPERTS):
            used = used_ref[bi, e]
            zero_rows((bi * N_EXPERTS + e) * capp + used, (capp - used) // ROW_ALIGN)

        @pl.when(bi == pl.num_programs(0) - 1)
        def _():
            zero_rows(trash, TRASH_ROWS // ROW_ALIGN)


def _dispatch(h_ext, ridx5, table3, goff3, nrows, used, capp, tt):
    b, t, dx = h_ext.shape
    nt = t // tt
    tr = tt // LANES
    pl_lanes = table3.shape[2]
    nxt = lambda bi, ti: jnp.minimum(bi * nt + ti + 1, b * nt - 1)
    grid_spec = pltpu.PrefetchScalarGridSpec(
        num_scalar_prefetch=2,
        grid=(b, nt),
        in_specs=[pl.BlockSpec((1, 1, pl_lanes), lambda bi, ti, nr, us: (bi * nt + ti, 0, 0),
                               memory_space=pltpu.SMEM),
                  pl.BlockSpec((1, 1, LANES), lambda bi, ti, nr, us: (bi * nt + ti, 0, 0),
                               memory_space=pltpu.SMEM),
                  pl.BlockSpec((1, 1, LANES), lambda bi, ti, nr, us: (nxt(bi, ti), 0, 0),
                               memory_space=pltpu.SMEM),
                  pl.BlockSpec((1, tt, dx), lambda bi, ti, nr, us: (bi, ti, 0)),
                  pl.BlockSpec((1, N_EXPERTS, 1, tr, LANES), lambda bi, ti, nr, us: (bi, 0, ti, 0, 0)),
                  pl.BlockSpec((1, N_EXPERTS, 1, tr, LANES),
                               lambda bi, ti, nr, us: (nxt(bi, ti) // nt, 0, nxt(bi, ti) % nt, 0, 0))],
        out_specs=pl.BlockSpec(memory_space=pl.ANY),
        scratch_shapes=[pltpu.VMEM((2, DISPATCH_CHUNKS * ROW_CHUNK, dx), BF16),
                        pltpu.VMEM((ROW_CHUNK, dx), BF16),
                        pltpu.VMEM((2, DISPATCH_CHUNKS, ROW_CHUNK, tt), BF16),
                        pltpu.VMEM((1, ROW_CHUNK, tt), BF16), pltpu.VMEM((ROW_ALIGN, dx), BF16),
                        pltpu.SemaphoreType.DMA((2,)), pltpu.SemaphoreType.DMA(()),
                        pltpu.SemaphoreType.DMA(())],
    )
    return pl.pallas_call(
        functools.partial(_dispatch_kernel, tr=tr, capp=capp),
        grid_spec=grid_spec,
        out_shape=jax.ShapeDtypeStruct((b * N_EXPERTS * capp + TRASH_ROWS, dx), BF16),
        compiler_params=_cparams("arbitrary", "arbitrary"),
    )(nrows, used, table3, goff3, goff3, h_ext, ridx5, ridx5)


def _ffn_kernel(used_ref, x_ref, wg_ref, wu_ref, wd_ref, o_ref, *, rb):
    e, bi = pl.program_id(0), pl.program_id(1)
    capp, d = o_ref.shape[2], o_ref.shape[3]
    nblk = (used_ref[bi, e] + (rb - 1)) // rb
    lane = lax.broadcasted_iota(I32, (1, LANES), 1)
    own = (lane == e) | (lane == e + N_EXPERTS) | (lane == e + 2 * N_EXPERTS)

    def body(j, carry):
        rows = pl.ds(pl.multiple_of(j * rb, rb), rb)
        xb = x_ref[rows, :]
        xm = xb[:, :d]
        val = jnp.sum(jnp.where(own, xb[:, d:].astype(F32), 0.0), axis=1, keepdims=True)
        hid = (_silu(_dot(xm, wg_ref[0, 0])) * _dot(xm, wu_ref[0, 0])).astype(BF16)
        o_ref[0, 0, rows, :] = (_dot(hid, wd_ref[0, 0]) * val).astype(BF16)
        return carry

    def zero(j, carry):
        o_ref[0, 0, pl.ds(pl.multiple_of(j * rb, rb), rb), :] = jnp.zeros((rb, d), BF16)
        return carry

    lax.fori_loop(0, nblk, body, 0)
    lax.fori_loop(nblk, capp // rb, zero, 0)


def _ffn(xe, used, wg, wu, wd, layer, capp, rb):
    b, e_n = used.shape
    dx = xe.shape[1]
    d, f = wg.shape[2], wg.shape[3]
    grid_spec = pltpu.PrefetchScalarGridSpec(
        num_scalar_prefetch=1,
        grid=(e_n, b),
        in_specs=[pl.BlockSpec((capp, dx), lambda e, bi, u: (bi * e_n + e, 0)),
                  pl.BlockSpec((1, 1, d, f), lambda e, bi, u: (layer, e, 0, 0)),
                  pl.BlockSpec((1, 1, d, f), lambda e, bi, u: (layer, e, 0, 0)),
                  pl.BlockSpec((1, 1, f, d), lambda e, bi, u: (layer, e, 0, 0))],
        out_specs=pl.BlockSpec((1, 1, capp, d), lambda e, bi, u: (bi, e, 0, 0)),
    )
    return pl.pallas_call(
        functools.partial(_ffn_kernel, rb=rb),
        grid_spec=grid_spec,
        out_shape=jax.ShapeDtypeStruct((b, e_n, capp, d), BF16),
        compiler_params=_cparams("arbitrary", "arbitrary"),
    )(used, xe, wg, wu, wd)


def _combine_kernel(nrows_ref, table_ref, next_table_ref, goff_ref, ridx_ref, x_ref, g2_ref, fg_ref,
                    ye_ref, o_ref, stage, extra, oh, acc, sem, xsem, *, tr, final):
    bi, ti = pl.program_id(0), pl.program_id(1)
    step = bi * pl.num_programs(1) + ti
    n_steps = pl.num_programs(0) * pl.num_programs(1)
    par = step % 2

    def chunks(s):
        return (nrows_ref[s] + (ROW_CHUNK - 1)) // ROW_CHUNK

    nch = chunks(step)
    n_pre = jnp.minimum(nch, COMBINE_AHEAD)

    def piece(tbl, c, p, dst, dsem):
        src = tbl[0, 0, c * PIECES + p]
        return pltpu.make_async_copy(
            ye_ref.at[pl.ds(pl.multiple_of(jnp.maximum(src, 0), ROW_ALIGN), ROW_ALIGN), :],
            dst.at[pl.ds(p * ROW_ALIGN, ROW_ALIGN), :], dsem)

    def fetch_tile(tbl, parity, n):
        for c in range(COMBINE_AHEAD):
            @pl.when(c < n)
            def _(c=c):
                for p in range(PIECES):
                    piece(tbl, c, p, stage.at[parity, c], sem.at[parity, c]).start()

    pl.when(step == 0)(lambda: fetch_tile(table_ref, 0, n_pre))

    @pl.when(step + 1 < n_steps)
    def _():
        fetch_tile(next_table_ref, 1 - par, jnp.minimum(chunks(step + 1), COMBINE_AHEAD))

    acc[...] = jnp.zeros(acc.shape, F32)
    span = _build_one_hot(oh, 0, ridx_ref, goff_ref, 0, tr)
    _patch_one_hot(oh, 0, ridx_ref, 0, *span, tr)

    def accumulate(c, rows):
        span = _build_one_hot(oh, (c + 1) % 2, ridx_ref, goff_ref, c + 1, tr)
        acc[...] += _dot_tn(oh[c % 2], rows[...])
        _patch_one_hot(oh, (c + 1) % 2, ridx_ref, c + 1, *span, tr)

    def prefetched(c, carry):
        for p in range(PIECES):
            piece(table_ref, c, p, stage.at[par, c], sem.at[par, c]).wait()
        accumulate(c, stage.at[par, c])
        return carry

    def on_the_spot(c, carry):
        for p in range(PIECES):
            piece(table_ref, c, p, extra, xsem).start()
        for p in range(PIECES):
            piece(table_ref, c, p, extra, xsem).wait()
        accumulate(c, extra)
        return carry

    lax.fori_loop(0, n_pre, prefetched, 0)
    lax.fori_loop(n_pre, nch, on_the_spot, 0)
    xo = x_ref[0] + g2_ref[0] * acc[...]
    if final:
        xo = xo * lax.rsqrt(jnp.mean(xo * xo, axis=-1, keepdims=True) + EPS) * fg_ref[...]
    o_ref[0] = xo


def _combine(ye_flat, ridx5, table3, goff3, nrows, x, g2, fg, tt, final):
    b, t, d = x.shape
    nt = t // tt
    tr = tt // LANES
    pl_lanes = table3.shape[2]
    grid_spec = pltpu.PrefetchScalarGridSpec(
        num_scalar_prefetch=1,
        grid=(b, nt),
        in_specs=[pl.BlockSpec((1, 1, pl_lanes), lambda bi, ti, nr: (bi * nt + ti, 0, 0),
                               memory_space=pltpu.SMEM),
                  pl.BlockSpec((1, 1, pl_lanes),
                               lambda bi, ti, nr: (jnp.minimum(bi * nt + ti + 1, b * nt - 1), 0, 0),
                               memory_space=pltpu.SMEM),
                  pl.BlockSpec((1, 1, LANES), lambda bi, ti, nr: (bi * nt + ti, 0, 0),
                               memory_space=pltpu.SMEM),
                  pl.BlockSpec((1, N_EXPERTS, 1, tr, LANES), lambda bi, ti, nr: (bi, 0, ti, 0, 0)),
                  pl.BlockSpec((1, tt, d), lambda bi, ti, nr: (bi, ti, 0)),
                  pl.BlockSpec((1, 1, d), lambda bi, ti, nr: (bi, 0, 0)),
                  pl.BlockSpec((1, d), lambda bi, ti, nr: (0, 0)),
                  pl.BlockSpec(memory_space=pl.ANY)],
        out_specs=pl.BlockSpec((1, tt, d), lambda bi, ti, nr: (bi, ti, 0)),
        scratch_shapes=[pltpu.VMEM((2, COMBINE_AHEAD, ROW_CHUNK, d), BF16),
                        pltpu.VMEM((ROW_CHUNK, d), BF16), pltpu.VMEM((2, ROW_CHUNK, tt), BF16),
                        pltpu.VMEM((tt, d), F32),
                        pltpu.SemaphoreType.DMA((2, COMBINE_AHEAD)), pltpu.SemaphoreType.DMA(())],
    )
    return pl.pallas_call(
        functools.partial(_combine_kernel, tr=tr, final=final),
        grid_spec=grid_spec,
        out_shape=jax.ShapeDtypeStruct((b, t, d), F32),
        compiler_params=_cparams("arbitrary", "arbitrary"),
    )(nrows.reshape(b * nt), table3, table3, goff3, ridx5, x, g2, fg, ye_flat)


def _moe_tiles(t):
    tt = min(512, t)
    t_sel = max(t, 8 * LANES)
    return tt, t_sel


def _ec_moe(x, g, shift, scale, gate, wr_pad, wg, wu, wd, layer, final_g, final):
    b, t, d = x.shape
    e_n = N_EXPERTS
    cap = EC_CAPACITY_FACTOR * t // e_n
    tt, t_sel = _moe_tiles(t)
    nt, tr = t // tt, tt // LANES
    capp = cap + ROW_ALIGN * nt
    rb = min(ROW_CHUNK, capp)
    capp = -(-capp // rb) * rb
    pl_lanes = -(-(tt + ROW_ALIGN) // LANES) * LANES

    h_ext, aff_t = _router(x, g, shift, scale, wr_pad, min(512, t))
    if t_sel > t:
        aff_t = jnp.pad(aff_t, ((0, 0), (0, 0), (0, t_sel - t)), constant_values=-1.0)
    aff_r = aff_t.reshape(b, e_n, t_sel // LANES, LANES)
    ridx, table, goff, used = _select(aff_r, cap, tr, capp, pl_lanes)
    ridx5 = ridx[:, :, :nt * tr].reshape(b, e_n, nt, tr, LANES)
    table3 = table[:, :nt * tr:tr].reshape(b * nt, 1, pl_lanes)
    goff3 = goff[:, :nt * tr:tr].reshape(b * nt, 1, LANES)
    nrows2 = goff[:, :nt * tr:tr, e_n]
    used2 = used[:, :, 0, 0]

    xe = _dispatch(h_ext, ridx5, table3, goff3, nrows2, used2, capp, tt)
    ye = _ffn(xe, used2, wg, wu, wd, layer, capp, rb)
    return _combine(ye.reshape(b * e_n * capp, d), ridx5, table3, goff3, nrows2, x, gate, final_g, tt,
                    final)


def kernel(x, c, ctx, c_ctx, ada_w, ada_b, norm_g, gla_w_in, gla_w_a2, gla_b_a2, gla_norm_g,
           gla_w_out, conv_w_in, conv_k, conv_w_out, router_w, expert_w_gate, expert_w_up,
           expert_w_down, final_norm_g):
    b, t, d = x.shape
    depth = ada_w.shape[0]
    assert depth == 2 and b + 1 <= 8
    dk = gla_w_a2.shape[3]
    dv = gla_w_out.shape[1]
    hk, hv = dk // GLA_HEADS, dv // GLA_HEADS

    cvec = jnp.zeros((8, d), F32).at[:b].set(c).at[b].set(c_ctx)
    mods = _ada(cvec, ada_w, ada_b)

    def mod_vectors(layer, rows):
        m = mods[layer, rows].reshape(rows.shape[0], 1, 6, d)
        return [m[:, :, i, :] for i in range(6)]

    rows_x = jnp.arange(b)
    rows_c = jnp.full((b,), b)
    wr_hi = router_w.astype(BF16)
    wr_lo = (router_w - wr_hi.astype(F32)).astype(BF16)
    wr_pad = [jnp.pad(jnp.concatenate([wr_hi[i], wr_lo[i]], axis=1), ((0, 0), (0, LANES - 2 * N_EXPERTS)))
              for i in range(depth)]
    fg = final_norm_g.reshape(1, d)

    sh1x, sc1x, g1x, sh2x, sc2x, g2x = mod_vectors(0, rows_x)
    sh1c, sc1c, g1c, sh2c, sc2c, g2c = mod_vectors(0, rows_c)
    n_main = 2 * dk + 2 * dv
    w_main = jnp.pad(gla_w_in[0], ((0, 0), (0, LANES - 2 * GLA_RANK))).astype(BF16)
    assert w_main.shape[1] == n_main + LANES
    w2 = jnp.zeros((2 * GLA_RANK, 2 * dk), F32)
    w2 = w2.at[:GLA_RANK, :dk].set(gla_w_a2[0, 0]).at[GLA_RANK:, dk:].set(gla_w_a2[0, 1])
    w2_hi = w2.astype(BF16)
    w2_lo = (w2 - w2_hi.astype(F32)).astype(BF16)
    w2 = jnp.concatenate([w2_hi, w2_hi, w2_lo, jnp.zeros((LANES - 6 * GLA_RANK, 2 * dk), BF16)])
    b2 = gla_b_a2[0].reshape(1, 2 * dk)
    gn = gla_norm_g[0].reshape(1, hv)
    w_out = gla_w_out[0].astype(BF16)
    g_n1 = norm_g[0, 0].reshape(1, d)
    g_n2 = norm_g[0, 1].reshape(1, d)

    tc = ctx.shape[1]
    qc, kc, vc, rc, lac = _gla_in(ctx, g_n1, sh1c, sc1c, w_main, w2, b2, min(512, tc))
    s0 = jnp.zeros((b, GLA_HEADS, hv, hk), F32)
    ocf, ocb, s_cf, s_cb = _gla_scan(qc, kc, vc, lac, s0, s0)
    qx, kx, vx, rx, lax_ = _gla_in(x, g_n1, sh1x, sc1x, w_main, w2, b2, 512)
    oxf, oxb, _, _, wg, wu, wd = _gla_scan(qx, kx, vx, lax_, s_cf, s_cb,
                                           cast=(expert_w_gate, expert_w_up, expert_w_down))
    wg, wu, wd = (w.reshape(s.shape) for w, s in
                  zip((wg, wu, wd), (expert_w_gate, expert_w_up, expert_w_down)))
    x = _gla_out(oxf, oxb, rx, x, gn, w_out, g1x, 512)
    x = _ec_moe(x, g_n2, sh2x, sc2x, g2x, wr_pad[0], wg, wu, wd, 0, fg, False)

    ctx = _gla_out(ocf, ocb, rc, ctx, gn, w_out, g1c, min(512, tc))
    ctx = _ec_moe(ctx, g_n2, sh2c, sc2c, g2c, wr_pad[0], wg, wu, wd, 0, fg, False)
    del ctx

    sh1x, sc1x, g1x, sh2x, sc2x, g2x = mod_vectors(1, rows_x)
    x = _conv_mix(x, norm_g[1, 0].reshape(1, d), sh1x, sc1x, conv_w_in[0].astype(BF16), conv_k[0],
                  conv_w_out[0].astype(BF16), g1x, GRID_W, 512)
    return _ec_moe(x, norm_g[1, 1].reshape(1, d), sh2x, sc2x, g2x, wr_pad[1], wg, wu, wd, 1, fg, True)
```

```python
import functools

import jax
import jax.numpy as jnp
from jax import lax
from jax.experimental import pallas as pl
from jax.experimental.pallas import tpu as pltpu

F32 = jnp.float32
BF16 = jnp.bfloat16
I32 = jnp.int32
HIGHEST = lax.Precision.HIGHEST

EPS = 1e-6
GRID_W = 64
GLA_HEADS = 4
GLA_RANK = 16
GLA_TAU = 16.0
N_EXPERTS = 16
EC_CAPACITY_FACTOR = 2
CONV_WIDTH = 3

LANES = 128
ROW_ALIGN = 16
GLA_CHUNK = 256
GLA_DIAG = 8
ROW_CHUNK = 256
PIECES = ROW_CHUNK // ROW_ALIGN
ONE_HOT_SLOTS = 3
DISPATCH_CHUNKS = 5
TRASH_ROWS = (2 * DISPATCH_CHUNKS + 1) * ROW_CHUNK
COMBINE_AHEAD = 5
MIN_NORMAL_BITS = 0x00800000
REFINE_STEPS = 40
VMEM_LIMIT = 60 * 1024 * 1024


def _cparams(*sem):
    return pltpu.CompilerParams(dimension_semantics=sem, vmem_limit_bytes=VMEM_LIMIT)


def _dot(a, b):
    return jnp.dot(a, b, preferred_element_type=F32)


def _dot_nt(a, b):
    return lax.dot_general(a, b, (((1,), (1,)), ((), ())), preferred_element_type=F32)


def _dot_tn(a, b):
    return lax.dot_general(a, b, (((0,), (0,)), ((), ())), preferred_element_type=F32)


def _xdot(a, b):
    return jnp.dot(a, b, precision=HIGHEST, preferred_element_type=F32)


def _split2(x):
    hi = x.astype(BF16)
    return hi, (x - hi.astype(F32)).astype(BF16)


def _silu(x):
    return x * jax.nn.sigmoid(x)


def _norm_mod(x, g, shift, scale):
    y = x * lax.rsqrt(jnp.mean(x * x, axis=-1, keepdims=True) + EPS) * g
    return y * (1.0 + scale) + shift


def _split3(x):
    hi = x.astype(BF16)
    r1 = x - hi.astype(F32)
    mid = r1.astype(BF16)
    lo = (r1 - mid.astype(F32)).astype(BF16)
    return hi, mid, lo


def _ada_kernel(c_ref, w_ref, b_ref, o_ref):
    o_ref[0] = _xdot(_silu(c_ref[...]), w_ref[0]) + b_ref[0]


def _ada(cvec, ada_w, ada_b):
    depth, d, n6 = ada_w.shape
    nt = n6 // d
    return pl.pallas_call(
        _ada_kernel,
        grid=(depth, nt),
        in_specs=[pl.BlockSpec((8, d), lambda l, n: (0, 0)),
                  pl.BlockSpec((1, d, d), lambda l, n: (l, 0, n)),
                  pl.BlockSpec((1, 1, d), lambda l, n: (l, 0, n))],
        out_specs=pl.BlockSpec((1, 8, d), lambda l, n: (l, 0, n)),
        out_shape=jax.ShapeDtypeStruct((depth, 8, n6), F32),
        compiler_params=_cparams("arbitrary", "arbitrary"),
    )(cvec, ada_w, ada_b.reshape(depth, 1, n6))


def _gla_in_kernel(x_ref, g_ref, sh_ref, sc_ref, w_ref, w2_ref, b2_ref,
                   q_ref, k_ref, v_ref, r_ref, la_ref, *, dk, dv):
    h = _norm_mod(x_ref[0], g_ref[...], sh_ref[0], sc_ref[0])
    y = _dot(h.astype(BF16), w_ref[...])
    hk = dk // GLA_HEADS
    q_ref[0] = y[:, :dk] * (hk ** -0.5)
    k_ref[0] = y[:, dk:2 * dk]
    v_ref[0] = y[:, 2 * dk:2 * dk + dv]
    r_ref[0] = y[:, 2 * dk + dv:2 * dk + 2 * dv].astype(BF16)
    a_hi, a_lo = _split2(y[:, 2 * dk + 2 * dv:])
    a3 = (a_hi.astype(F32) + pltpu.roll(a_lo.astype(F32), 2 * GLA_RANK, 1)
          + pltpu.roll(a_hi.astype(F32), 4 * GLA_RANK, 1))
    z = _dot(a3.astype(BF16), w2_ref[...]) + b2_ref[...]
    log_sig = jnp.minimum(z, 0.0) - jnp.log(1.0 + jnp.exp(-jnp.abs(z)))
    la_ref[0] = log_sig * (1.0 / GLA_TAU)


def _gla_in(x, g, shift, scale, w_main, w2, b2, tm):
    b, t, d = x.shape
    dk = w2.shape[1] // 2
    dv = (w_main.shape[1] - LANES - 2 * dk) // 2
    nw = w_main.shape[1]
    tok = lambda width: pl.BlockSpec((1, tm, width), lambda bi, ti: (bi, ti, 0))
    vec = pl.BlockSpec((1, 1, d), lambda bi, ti: (bi, 0, 0))
    full = lambda shape: pl.BlockSpec(shape, lambda bi, ti: tuple(0 for _ in shape))
    return pl.pallas_call(
        functools.partial(_gla_in_kernel, dk=dk, dv=dv),
        grid=(b, t // tm),
        in_specs=[tok(d), full((1, d)), vec, vec, full((d, nw)), full((LANES, 2 * dk)), full((1, 2 * dk))],
        out_specs=[tok(dk), tok(dk), tok(dv), tok(dv), tok(2 * dk)],
        out_shape=[jax.ShapeDtypeStruct((b, t, dk), F32), jax.ShapeDtypeStruct((b, t, dk), F32),
                   jax.ShapeDtypeStruct((b, t, dv), F32), jax.ShapeDtypeStruct((b, t, dv), BF16),
                   jax.ShapeDtypeStruct((b, t, 2 * dk), F32)],
        compiler_params=_cparams("arbitrary", "arbitrary"),
    )(x, g, shift, scale, w_main, w2, b2)


def _gla_chunk(q, k, v, la, st, rev):
    c, hk = q.shape
    ii = lax.broadcasted_iota(I32, (c, c), 0)
    jj = lax.broadcasted_iota(I32, (c, c), 1)
    tri = ((jj >= ii) if rev else (jj <= ii)).astype(BF16)
    hi, mid, lo = _split3(la)
    acum = _dot(tri, hi) + _dot(tri, mid) + _dot(tri, lo)
    a_last = acum[0:1] if rev else acum[c - 1:c]

    o = _dot_nt((q * jnp.exp(acum)).astype(BF16), st.astype(BF16))

    xor = ii ^ jj
    rowi = lax.broadcasted_iota(I32, (c, hk), 0)
    p = jnp.zeros((c, c), F32)
    blk = c
    while blk >= 2 * GLA_DIAG:
        half = blk // 2
        a3 = acum.reshape(c // blk, blk, hk)
        ridx = half if rev else half - 1
        ref = jnp.broadcast_to(a3[:, ridx:ridx + 1, :], a3.shape).reshape(c, hk)
        pos = rowi & (blk - 1)
        late = (pos < half) if rev else (pos >= half)
        qe = jnp.where(late, q * jnp.exp(acum - ref), 0.0).astype(BF16)
        ke = jnp.where(late, 0.0, k * jnp.exp(ref - acum)).astype(BF16)
        raw = _dot_nt(qe, ke)
        p = p + (raw if blk == c else jnp.where(xor < blk, raw, 0.0))
        blk = half

    nb = c // GLA_DIAG
    q8 = q.reshape(nb, GLA_DIAG, hk)
    k8 = k.reshape(nb, GLA_DIAG, hk)
    a8 = acum.reshape(nb, GLA_DIAG, hk)
    sub = lax.broadcasted_iota(I32, (nb, GLA_DIAG, hk), 1)
    colmod = jj & (GLA_DIAG - 1)
    pd = jnp.zeros((c, c), F32)
    for j in range(GLA_DIAG):
        kj = jnp.broadcast_to(k8[:, j:j + 1, :], k8.shape)
        aj = jnp.broadcast_to(a8[:, j:j + 1, :], a8.shape)
        valid = (sub <= j) if rev else (sub >= j)
        dec = jnp.where(valid, jnp.exp(a8 - aj), 0.0)
        cj = jnp.sum(q8 * kj * dec, axis=2, keepdims=True).reshape(c, 1)
        pd = jnp.where(colmod == j, cj, pd)
    p = p + jnp.where(xor < GLA_DIAG, pd, 0.0)

    vb = v.astype(BF16)
    o = o + _dot(p.astype(BF16), vb)
    kd = (k * jnp.exp(a_last - acum)).astype(BF16)
    st_new = st * jnp.exp(a_last) + _dot_tn(vb, kd)
    return o, st_new


def _gla_scan_kernel(*refs, n_cast):
    (qf_ref, kf_ref, vf_ref, laf_ref, qb_ref, kb_ref, vb_ref, lab_ref, s0f_ref, s0b_ref) = refs[:10]
    w_in = refs[10:10 + n_cast]
    of_ref, ob_ref, sf_ref, sb_ref = refs[10 + n_cast:14 + n_cast]
    w_out = refs[14 + n_cast:14 + 2 * n_cast]
    stf, stb = refs[14 + 2 * n_cast:]
    n = pl.program_id(2)
    for wi, wo in zip(w_in, w_out):
        wo[0] = wi[0, 0].astype(BF16)

    @pl.when(n == 0)
    def _():
        stf[...] = s0f_ref[0, 0]
        stb[...] = s0b_ref[0, 0]

    o, s = _gla_chunk(qf_ref[0], kf_ref[0], vf_ref[0], laf_ref[0], stf[...], False)
    of_ref[0] = o.astype(BF16)
    stf[...] = s
    o, s = _gla_chunk(qb_ref[0], kb_ref[0], vb_ref[0], lab_ref[0], stb[...], True)
    ob_ref[0] = o.astype(BF16)
    stb[...] = s

    @pl.when(n == pl.num_programs(2) - 1)
    def _():
        sf_ref[0, 0] = stf[...]
        sb_ref[0, 0] = stb[...]


def _gla_scan(q, k, v, la, s0f, s0b, cast=()):
    b, t, dk = q.shape
    dv = v.shape[2]
    hk, hv = dk // GLA_HEADS, dv // GLA_HEADS
    c = GLA_CHUNK
    nc = t // c
    steps = b * GLA_HEADS * nc
    fwd = lambda w, off: pl.BlockSpec((1, c, w), lambda bi, h, n: (bi, n, h + off))
    bwd = lambda w, off: pl.BlockSpec((1, c, w), lambda bi, h, n: (bi, nc - 1 - n, h + off))
    st = pl.BlockSpec((1, 1, hv, hk), lambda bi, h, n: (bi, h, 0, 0))
    cast_in, cast_out, cast_shape = [], [], []
    for w in cast:
        n_l, n_e, rows, cols = w.shape
        per = steps // (n_l * n_e)
        rb = rows // per
        assert per * n_l * n_e == steps and rb * per == rows and rb % ROW_ALIGN == 0
        step = lambda bi, h, n: (bi * GLA_HEADS + h) * nc + n
        cast_in.append(pl.BlockSpec(
            (1, 1, rb, cols),
            lambda bi, h, n, per=per, n_e=n_e: (step(bi, h, n) // per // n_e, step(bi, h, n) // per % n_e,
                                                step(bi, h, n) % per, 0)))
        cast_out.append(pl.BlockSpec(
            (1, rb, cols), lambda bi, h, n, per=per: (step(bi, h, n) // per, step(bi, h, n) % per, 0)))
        cast_shape.append(jax.ShapeDtypeStruct((n_l * n_e, rows, cols), BF16))
    return pl.pallas_call(
        functools.partial(_gla_scan_kernel, n_cast=len(cast)),
        grid=(b, GLA_HEADS, nc),
        in_specs=[fwd(hk, 0), fwd(hk, 0), fwd(hv, 0), fwd(hk, 0),
                  bwd(hk, 0), bwd(hk, 0), bwd(hv, 0), bwd(hk, GLA_HEADS), st, st] + cast_in,
        out_specs=[fwd(hv, 0), bwd(hv, 0), st, st] + cast_out,
        out_shape=[jax.ShapeDtypeStruct((b, t, dv), BF16), jax.ShapeDtypeStruct((b, t, dv), BF16),
                   jax.ShapeDtypeStruct((b, GLA_HEADS, hv, hk), F32),
                   jax.ShapeDtypeStruct((b, GLA_HEADS, hv, hk), F32)] + cast_shape,
        scratch_shapes=[pltpu.VMEM((hv, hk), F32), pltpu.VMEM((hv, hk), F32)],
        compiler_params=_cparams("arbitrary", "arbitrary", "arbitrary"),
    )(q, k, v, la, q, k, v, la, s0f, s0b, *cast)


def _gla_out_kernel(of_ref, ob_ref, r_ref, x_ref, gn_ref, w_ref, g1_ref, o_ref):
    o = of_ref[0].astype(F32) + ob_ref[0].astype(F32)
    hv = gn_ref.shape[1]
    parts = []
    for h in range(GLA_HEADS):
        oh = o[:, h * hv:(h + 1) * hv]
        ms = jnp.mean(oh * oh, axis=-1, keepdims=True)
        parts.append(oh * lax.rsqrt(ms + EPS) * gn_ref[...])
    y = jnp.concatenate(parts, axis=1) * _silu(r_ref[0].astype(F32))
    o_ref[0] = x_ref[0] + g1_ref[0] * _dot(y.astype(BF16), w_ref[...])


def _gla_out(o_f, o_b, r, x, gn, w_out, g1, tm):
    b, t, d = x.shape
    dv = o_f.shape[2]
    tok = lambda width: pl.BlockSpec((1, tm, width), lambda bi, ti: (bi, ti, 0))
    return pl.pallas_call(
        _gla_out_kernel,
        grid=(b, t // tm),
        in_specs=[tok(dv), tok(dv), tok(dv), tok(d),
                  pl.BlockSpec((1, dv // GLA_HEADS), lambda bi, ti: (0, 0)),
                  pl.BlockSpec((dv, d), lambda bi, ti: (0, 0)),
                  pl.BlockSpec((1, 1, d), lambda bi, ti: (bi, 0, 0))],
        out_specs=tok(d),
        out_shape=jax.ShapeDtypeStruct((b, t, d), F32),
        compiler_params=_cparams("arbitrary", "arbitrary"),
    )(o_f, o_b, r, x, gn, w_out, g1)


def _conv_mix_kernel(x_ref, g_ref, sh_ref, sc_ref, win_ref, ck_ref, wout_ref, g1_ref, o_ref, *, seg):
    x = x_ref[0]
    tm, d = x.shape
    h = _norm_mod(x, g_ref[...], sh_ref[0], sc_ref[0])
    y = _dot(h.astype(BF16), win_ref[...])
    bg, cg, v = y[:, :d], y[:, d:2 * d], y[:, 2 * d:]
    u = cg * v
    pos = lax.broadcasted_iota(I32, (tm, 1), 0) % seg
    u_prev = jnp.where(pos == 0, 0.0, pltpu.roll(u, 1, 0))
    u_next = jnp.where(pos == seg - 1, 0.0, pltpu.roll(u, tm - 1, 0))
    conv = u_prev * ck_ref[0:1, :] + u * ck_ref[1:2, :] + u_next * ck_ref[2:3, :]
    o_ref[0] = x + g1_ref[0] * _dot((bg * conv).astype(BF16), wout_ref[...])


def _conv_mix(x, g, shift, scale, w_in, ck, w_out, g1, seg, tm):
    b, t, d = x.shape
    tok = pl.BlockSpec((1, tm, d), lambda bi, ti: (bi, ti, 0))
    vec = pl.BlockSpec((1, 1, d), lambda bi, ti: (bi, 0, 0))
    full = lambda shape: pl.BlockSpec(shape, lambda bi, ti: tuple(0 for _ in shape))
    return pl.pallas_call(
        functools.partial(_conv_mix_kernel, seg=seg),
        grid=(b, t // tm),
        in_specs=[tok, full((1, d)), vec, vec, full((d, 3 * d)), full((CONV_WIDTH, d)), full((d, d)), vec],
        out_specs=tok,
        out_shape=jax.ShapeDtypeStruct((b, t, d), F32),
        compiler_params=_cparams("arbitrary", "arbitrary"),
    )(x, g, shift, scale, w_in, ck, w_out, g1)


def _router_kernel(x_ref, g_ref, sh_ref, sc_ref, wr_ref, h_ref, aff_ref):
    d = x_ref.shape[2]
    h = _norm_mod(x_ref[0], g_ref[...], sh_ref[0], sc_ref[0])
    h_hi, h_lo = _split2(h)
    p = _dot(h_hi, wr_ref[...]) + _dot(h_lo, wr_ref[...])
    logits = p + pltpu.roll(p, LANES - N_EXPERTS, 1)
    lane = lax.broadcasted_iota(I32, logits.shape, 1)
    is_e = lane < N_EXPERTS
    m = jnp.max(jnp.where(is_e, logits, -jnp.inf), axis=-1, keepdims=True)
    ex = jnp.where(is_e, jnp.exp(logits - m), 0.0)
    aff = ex / jnp.sum(ex, axis=-1, keepdims=True)
    aff_ref[0] = aff.T[:N_EXPERTS, :]
    hi, mid, lo = _split3(aff)
    ext = (hi.astype(F32) + pltpu.roll(mid.astype(F32), N_EXPERTS, 1)
           + pltpu.roll(lo.astype(F32), 2 * N_EXPERTS, 1))
    h_ref[0, :, :d] = h.astype(BF16)
    h_ref[0, :, d:] = ext.astype(BF16)


def _router(x, g, shift, scale, wr_pad, tm):
    b, t, d = x.shape
    tok = lambda width: pl.BlockSpec((1, tm, width), lambda bi, ti: (bi, ti, 0))
    vec = pl.BlockSpec((1, 1, d), lambda bi, ti: (bi, 0, 0))
    return pl.pallas_call(
        _router_kernel,
        grid=(b, t // tm),
        in_specs=[tok(d), pl.BlockSpec((1, d), lambda bi, ti: (0, 0)), vec, vec,
                  pl.BlockSpec((d, LANES), lambda bi, ti: (0, 0))],
        out_specs=[tok(d + LANES), pl.BlockSpec((1, N_EXPERTS, tm), lambda bi, ti: (bi, 0, ti))],
        out_shape=[jax.ShapeDtypeStruct((b, t, d + LANES), BF16),
                   jax.ShapeDtypeStruct((b, N_EXPERTS, t), F32)],
        compiler_params=_cparams("arbitrary", "arbitrary"),
    )(x, g, shift, scale, wr_pad)


def _select_kernel(aff_ref, ridx_ref, table_ref, goff_ref, used_ref, *, cap, tr, capp, pl_lanes):
    e_n = N_EXPERTS
    a = aff_ref[0]
    r_n = a.shape[1]

    def count_ge(v):
        return jnp.sum(jnp.where(a >= v, 1.0, 0.0), axis=(1, 2), keepdims=True)

    def search(it, cur):
        cand = cur | (jnp.int32(1) << (30 - it))
        return jnp.where(count_ge(lax.bitcast_convert_type(cand, F32)) >= cap, cand, cur)

    thr = lax.fori_loop(0, 31, search, jnp.zeros((e_n, 1, 1), I32))
    normal = thr >= MIN_NORMAL_BITS
    lo = lax.bitcast_convert_type(jnp.where(normal, thr, 0), F32)
    hi = lax.bitcast_convert_type(jnp.where(normal, thr + 1, MIN_NORMAL_BITS), F32)

    def refine(it, lh):
        lo_, hi_ = lh
        mid = 0.5 * (lo_ + hi_)
        ok = count_ge(mid) >= cap
        return jnp.where(ok, mid, lo_), jnp.where(ok, hi_, mid)

    lo, hi = lax.fori_loop(0, REFINE_STEPS, refine, (lo, hi))
    gt = a >= hi
    eq = (a >= lo) & jnp.logical_not(gt)
    need = cap - jnp.sum(jnp.where(gt, 1.0, 0.0), axis=(1, 2), keepdims=True)

    li = lax.broadcasted_iota(I32, (LANES, LANES), 0)
    lj = lax.broadcasted_iota(I32, (LANES, LANES), 1)
    upper = (li <= lj).astype(BF16)
    ones = jnp.ones((LANES, LANES), BF16)
    ri = lax.broadcasted_iota(I32, (r_n, r_n), 0)
    rj = lax.broadcasted_iota(I32, (r_n, r_n), 1)
    strict_lower = (rj < ri).astype(BF16)

    def cumsum_tokens(mask):
        m = jnp.where(mask, 1.0, 0.0).astype(BF16).reshape(e_n * r_n, LANES)
        within = _dot(m, upper).reshape(e_n, r_n, LANES)
        rs = _dot(m, ones).reshape(e_n, r_n, LANES)
        pre = jnp.stack([_dot(strict_lower, rs[e].astype(BF16)) for e in range(e_n)])
        return within + pre, rs

    eq_rank, _ = cumsum_tokens(eq)
    sel = gt | (eq & (eq_rank <= need))
    incl, rs = cumsum_tokens(sel)
    pos = incl - jnp.where(sel, 1.0, 0.0)

    same_tile = ((ri // tr) == (rj // tr)).astype(F32)
    prev_tile = ((rj // tr) < (ri // tr)).astype(F32)
    first_row = (rj % tr) == 0
    prev_first = jnp.where(first_row, prev_tile, 0.0)
    all_first = jnp.where(first_row, 1.0, 0.0)

    goff = jnp.zeros((r_n, LANES), F32)
    lane = lax.broadcasted_iota(I32, (r_n, LANES), 1)
    gtab = jnp.zeros((r_n, LANES), F32)
    rows = []
    tbl = jnp.full((r_n, pl_lanes), -1.0, F32)
    row16 = (lax.broadcasted_iota(I32, (r_n, pl_lanes), 1) * ROW_ALIGN).astype(F32)
    rep = lambda x: jnp.concatenate([x] * (pl_lanes // LANES), axis=1)
    base = (pl.program_id(0) * e_n * capp).astype(F32)
    for e in range(e_n):
        n_tile = _xdot(same_tile, rs[e])
        pos_start = _xdot(prev_tile, rs[e])
        n_pad = jnp.floor((n_tile + (ROW_ALIGN - 1)) * (1.0 / ROW_ALIGN)) * ROW_ALIGN
        off = _xdot(prev_first, n_pad)
        used_ref[0, e] = _xdot(all_first, n_pad).astype(I32)
        rows.append(jnp.where(sel[e], goff + pos[e] - pos_start, -1.0))
        gtab = jnp.where(lane == e, goff, gtab)
        g0, g1 = rep(goff), rep(goff + n_pad)
        inside = (row16 >= g0) & (row16 < g1)
        tbl = jnp.where(inside, base + e * capp + rep(off) + row16 - g0, tbl)
        goff = goff + n_pad
    ridx_ref[0] = jnp.stack(rows).astype(I32)
    table_ref[0] = tbl.astype(I32)
    goff_ref[0] = jnp.where(lane >= e_n, goff, gtab).astype(I32)


def _select(aff_r, cap, tr, capp, pl_lanes):
    b, e_n, r_n, _ = aff_r.shape
    blk4 = pl.BlockSpec((1, e_n, r_n, LANES), lambda bi: (bi, 0, 0, 0))
    return pl.pallas_call(
        functools.partial(_select_kernel, cap=cap, tr=tr, capp=capp, pl_lanes=pl_lanes),
        grid=(b,),
        in_specs=[blk4],
        out_specs=[blk4, pl.BlockSpec((1, r_n, pl_lanes), lambda bi: (bi, 0, 0)),
                   pl.BlockSpec((1, r_n, LANES), lambda bi: (bi, 0, 0)), blk4],
        out_shape=[jax.ShapeDtypeStruct((b, e_n, r_n, LANES), I32),
                   jax.ShapeDtypeStruct((b, r_n, pl_lanes), I32),
                   jax.ShapeDtypeStruct((b, r_n, LANES), I32),
                   jax.ShapeDtypeStruct((b, e_n, r_n, LANES), I32)],
        compiler_params=_cparams("arbitrary"),
    )(aff_r)


def _expert_span(goff_ref, base):
    e_lo = jnp.int32(0)
    e_hi = jnp.int32(0)
    for e in range(N_EXPERTS):
        e_lo += (goff_ref[0, 0, e + 1] <= base).astype(I32)
        e_hi += (goff_ref[0, 0, e] < base + ROW_CHUNK).astype(I32)
    return e_lo, e_hi


def _one_hot_rows(ridx_ref, base, e_lo, e_hi, tr):
    riota = lax.broadcasted_iota(I32, (ROW_CHUNK, LANES), 0) + base
    pieces = []
    for q in range(tr):
        acc = jnp.zeros((ROW_CHUNK, LANES), F32)
        for k in range(ONE_HOT_SLOTS):
            e = jnp.minimum(e_lo + k, N_EXPERTS - 1)
            row = jnp.where(e_lo + k < e_hi, ridx_ref[0, e, 0, pl.ds(q, 1), :], -2)
            acc = jnp.where(row == riota, 1.0, acc)
        pieces.append(acc.astype(BF16))
    return jnp.concatenate(pieces, axis=1)


def _one_hot_rows_any(ridx_ref, base, e_lo, e_hi, tr):
    riota = lax.broadcasted_iota(I32, (ROW_CHUNK, LANES), 0) + base
    pieces = []
    for q in range(tr):
        def add_expert(e, acc, q=q):
            return jnp.where(ridx_ref[0, e, 0, pl.ds(q, 1), :] == riota, 1.0, acc)
        acc = lax.fori_loop(e_lo, e_hi, add_expert, jnp.zeros((ROW_CHUNK, LANES), F32))
        pieces.append(acc.astype(BF16))
    return jnp.concatenate(pieces, axis=1)


def _build_one_hot(oh_ref, slot, ridx_ref, goff_ref, c, tr):
    base = c * ROW_CHUNK
    e_lo, e_hi = _expert_span(goff_ref, base)
    oh_ref[slot] = _one_hot_rows(ridx_ref, base, e_lo, e_hi, tr)
    return e_lo, e_hi


def _patch_one_hot(oh_ref, slot, ridx_ref, c, e_lo, e_hi, tr):
    @pl.when(e_hi - e_lo > ONE_HOT_SLOTS)
    def _():
        oh_ref[slot] = _one_hot_rows_any(ridx_ref, c * ROW_CHUNK, e_lo, e_hi, tr)


def _dispatch_kernel(nrows_ref, used_ref, table_ref, goff_ref, next_goff_ref, x_ref, ridx_ref,
                     next_ridx_ref, xe_ref, stage, extra, oh, oh_extra, zeros, sem, xsem, zsem,
                     *, tr, capp):
    bi, ti = pl.program_id(0), pl.program_id(1)
    step = bi * pl.num_programs(1) + ti
    n_steps = pl.num_programs(0) * pl.num_programs(1)
    par = step % 2
    nch = (nrows_ref[bi, ti] + (ROW_CHUNK - 1)) // ROW_CHUNK
    x = x_ref[0]
    trash = pl.num_programs(0) * N_EXPERTS * capp
    n_big = DISPATCH_CHUNKS * PIECES

    def piece(src, j, dst, dsem):
        return pltpu.make_async_copy(
            src.at[pl.ds(j * ROW_ALIGN, ROW_ALIGN), :],
            xe_ref.at[pl.ds(pl.multiple_of(dst, ROW_ALIGN), ROW_ALIGN), :], dsem)

    def wait_big(parity):
        for j in range(n_big):
            piece(stage.at[parity], j, 0, sem.at[parity]).wait()

    def build_tile(rref, gref, parity):
        return [_build_one_hot(oh.at[parity], c, rref, gref, c, tr) for c in range(DISPATCH_CHUNKS)]

    def patch_tile(rref, parity, spans):
        for c, span in enumerate(spans):
            _patch_one_hot(oh.at[parity], c, rref, c, *span, tr)

    pl.when(step == 0)(lambda: patch_tile(ridx_ref, 0, build_tile(ridx_ref, goff_ref, 0)))
    pl.when(step >= 2)(lambda: wait_big(par))

    next_spans = build_tile(next_ridx_ref, next_goff_ref, 1 - par)
    rows = DISPATCH_CHUNKS * ROW_CHUNK
    stage[par] = _dot(oh[par].reshape(rows, oh.shape[3]), x).astype(BF16)
    for j in range(n_big):
        dst = table_ref[0, 0, j]
        dst = jnp.where(dst >= 0, dst, trash + (par * n_big + j) * ROW_ALIGN)
        piece(stage.at[par], j, dst, sem.at[par]).start()
    patch_tile(next_ridx_ref, 1 - par, next_spans)

    def one_more(c, carry):
        span = _build_one_hot(oh_extra, 0, ridx_ref, goff_ref, c, tr)
        _patch_one_hot(oh_extra, 0, ridx_ref, c, *span, tr)
        extra[...] = _dot(oh_extra[0], x).astype(BF16)
        for p in range(PIECES):
            dst = table_ref[0, 0, c * PIECES + p]
            dst = jnp.where(dst >= 0, dst, trash + (2 * n_big + p) * ROW_ALIGN)
            piece(extra, p, dst, xsem).start()
        for p in range(PIECES):
            piece(extra, p, 0, xsem).wait()
        return carry

    lax.fori_loop(DISPATCH_CHUNKS, nch, one_more, 0)

    @pl.when(step == n_steps - 1)
    def _():
        wait_big(par)
        pl.when(n_steps >= 2)(lambda: wait_big(1 - par))

    def zero_piece(row):
        return pltpu.make_async_copy(
            zeros, xe_ref.at[pl.ds(pl.multiple_of(row, ROW_ALIGN), ROW_ALIGN), :], zsem)

    def zero_rows(first, n_pieces):
        def start(k, carry):
            zero_piece(first + k * ROW_ALIGN).start()
            return carry

        def wait(k, carry):
            zero_piece(first + k * ROW_ALIGN).wait()
            return carry

        lax.fori_loop(0, n_pieces, start, 0)
        lax.fori_loop(0, n_pieces, wait, 0)

    @pl.when(ti == pl.num_programs(1) - 1)
    def _():
        zeros[...] = jnp.zeros(zeros.shape, BF16)
        for e in range(N_EXPERTS):
            used = used_ref[bi, e]
            zero_rows((bi * N_EXPERTS + e) * capp + used, (capp - used) // ROW_ALIGN)

        @pl.when(bi == pl.num_programs(0) - 1)
        def _():
            zero_rows(trash, TRASH_ROWS // ROW_ALIGN)


def _dispatch(h_ext, ridx5, table3, goff3, nrows, used, capp, tt):
    b, t, dx = h_ext.shape
    nt = t // tt
    tr = tt // LANES
    pl_lanes = table3.shape[2]
    nxt = lambda bi, ti: jnp.minimum(bi * nt + ti + 1, b * nt - 1)
    grid_spec = pltpu.PrefetchScalarGridSpec(
        num_scalar_prefetch=2,
        grid=(b, nt),
        in_specs=[pl.BlockSpec((1, 1, pl_lanes), lambda bi, ti, nr, us: (bi * nt + ti, 0, 0),
                               memory_space=pltpu.SMEM),
                  pl.BlockSpec((1, 1, LANES), lambda bi, ti, nr, us: (bi * nt + ti, 0, 0),
                               memory_space=pltpu.SMEM),
                  pl.BlockSpec((1, 1, LANES), lambda bi, ti, nr, us: (nxt(bi, ti), 0, 0),
                               memory_space=pltpu.SMEM),
                  pl.BlockSpec((1, tt, dx), lambda bi, ti, nr, us: (bi, ti, 0)),
                  pl.BlockSpec((1, N_EXPERTS, 1, tr, LANES), lambda bi, ti, nr, us: (bi, 0, ti, 0, 0)),
                  pl.BlockSpec((1, N_EXPERTS, 1, tr, LANES),
                               lambda bi, ti, nr, us: (nxt(bi, ti) // nt, 0, nxt(bi, ti) % nt, 0, 0))],
        out_specs=pl.BlockSpec(memory_space=pl.ANY),
        scratch_shapes=[pltpu.VMEM((2, DISPATCH_CHUNKS * ROW_CHUNK, dx), BF16),
                        pltpu.VMEM((ROW_CHUNK, dx), BF16),
                        pltpu.VMEM((2, DISPATCH_CHUNKS, ROW_CHUNK, tt), BF16),
                        pltpu.VMEM((1, ROW_CHUNK, tt), BF16), pltpu.VMEM((ROW_ALIGN, dx), BF16),
                        pltpu.SemaphoreType.DMA((2,)), pltpu.SemaphoreType.DMA(()),
                        pltpu.SemaphoreType.DMA(())],
    )
    return pl.pallas_call(
        functools.partial(_dispatch_kernel, tr=tr, capp=capp),
        grid_spec=grid_spec,
        out_shape=jax.ShapeDtypeStruct((b * N_EXPERTS * capp + TRASH_ROWS, dx), BF16),
        compiler_params=_cparams("arbitrary", "arbitrary"),
    )(nrows, used, table3, goff3, goff3, h_ext, ridx5, ridx5)


def _ffn_kernel(used_ref, x_ref, wg_ref, wu_ref, wd_ref, o_ref, *, rb):
    e, bi = pl.program_id(0), pl.program_id(1)
    capp, d = o_ref.shape[2], o_ref.shape[3]
    nblk = (used_ref[bi, e] + (rb - 1)) // rb
    lane = lax.broadcasted_iota(I32, (1, LANES), 1)
    own = (lane == e) | (lane == e + N_EXPERTS) | (lane == e + 2 * N_EXPERTS)

    def body(j, carry):
        rows = pl.ds(pl.multiple_of(j * rb, rb), rb)
        xb = x_ref[rows, :]
        xm = xb[:, :d]
        val = jnp.sum(jnp.where(own, xb[:, d:].astype(F32), 0.0), axis=1, keepdims=True)
        hid = (_silu(_dot(xm, wg_ref[0, 0])) * _dot(xm, wu_ref[0, 0])).astype(BF16)
        o_ref[0, 0, rows, :] = (_dot(hid, wd_ref[0, 0]) * val).astype(BF16)
        return carry

    def zero(j, carry):
        o_ref[0, 0, pl.ds(pl.multiple_of(j * rb, rb), rb), :] = jnp.zeros((rb, d), BF16)
        return carry

    lax.fori_loop(0, nblk, body, 0)
    lax.fori_loop(nblk, capp // rb, zero, 0)


def _ffn(xe, used, wg, wu, wd, layer, capp, rb):
    b, e_n = used.shape
    dx = xe.shape[1]
    d, f = wg.shape[2], wg.shape[3]
    grid_spec = pltpu.PrefetchScalarGridSpec(
        num_scalar_prefetch=1,
        grid=(e_n, b),
        in_specs=[pl.BlockSpec((capp, dx), lambda e, bi, u: (bi * e_n + e, 0)),
                  pl.BlockSpec((1, 1, d, f), lambda e, bi, u: (layer, e, 0, 0)),
                  pl.BlockSpec((1, 1, d, f), lambda e, bi, u: (layer, e, 0, 0)),
                  pl.BlockSpec((1, 1, f, d), lambda e, bi, u: (layer, e, 0, 0))],
        out_specs=pl.BlockSpec((1, 1, capp, d), lambda e, bi, u: (bi, e, 0, 0)),
    )
    return pl.pallas_call(
        functools.partial(_ffn_kernel, rb=rb),
        grid_spec=grid_spec,
        out_shape=jax.ShapeDtypeStruct((b, e_n, capp, d), BF16),
        compiler_params=_cparams("arbitrary", "arbitrary"),
    )(used, xe, wg, wu, wd)


def _combine_kernel(nrows_ref, table_ref, next_table_ref, goff_ref, next_goff_ref, ridx_ref,
                    next_ridx_ref, x_ref, g2_ref, fg_ref, ye_ref, o_ref,
                    stage, extra, oh, oh_extra, acc, sem, xsem, *, tr, final):
    bi, ti = pl.program_id(0), pl.program_id(1)
    step = bi * pl.num_programs(1) + ti
    n_steps = pl.num_programs(0) * pl.num_programs(1)
    par = step % 2

    def chunks(s):
        return (nrows_ref[s] + (ROW_CHUNK - 1)) // ROW_CHUNK

    nch = chunks(step)
    n_pre = jnp.minimum(nch, COMBINE_AHEAD)

    def piece(tbl, c, p, dst, dsem):
        src = tbl[0, 0, c * PIECES + p]
        return pltpu.make_async_copy(
            ye_ref.at[pl.ds(pl.multiple_of(jnp.maximum(src, 0), ROW_ALIGN), ROW_ALIGN), :],
            dst.at[pl.ds(p * ROW_ALIGN, ROW_ALIGN), :], dsem)

    def fetch_tile(tbl, parity, n):
        for c in range(COMBINE_AHEAD):
            @pl.when(c < n)
            def _(c=c):
                for p in range(PIECES):
                    piece(tbl, c, p, stage.at[parity, c], sem.at[parity, c]).start()

    def build_tile(rref, gref, parity):
        return [_build_one_hot(oh.at[parity], c, rref, gref, c, tr) for c in range(COMBINE_AHEAD)]

    def patch_tile(rref, parity, spans):
        for c, span in enumerate(spans):
            _patch_one_hot(oh.at[parity], c, rref, c, *span, tr)

    @pl.when(step == 0)
    def _():
        stage[...] = jnp.zeros(stage.shape, BF16)
        fetch_tile(table_ref, 0, n_pre)
        patch_tile(ridx_ref, 0, build_tile(ridx_ref, goff_ref, 0))

    @pl.when(step + 1 < n_steps)
    def _():
        fetch_tile(next_table_ref, 1 - par, jnp.minimum(chunks(step + 1), COMBINE_AHEAD))

    for c in range(COMBINE_AHEAD):
        @pl.when(c < n_pre)
        def _(c=c):
            for p in range(PIECES):
                piece(table_ref, c, p, stage.at[par, c], sem.at[par, c]).wait()

    next_spans = build_tile(next_ridx_ref, next_goff_ref, 1 - par)
    rows = COMBINE_AHEAD * ROW_CHUNK
    acc[...] = _dot_tn(oh[par].reshape(rows, oh.shape[3]), stage[par].reshape(rows, stage.shape[3]))
    patch_tile(next_ridx_ref, 1 - par, next_spans)

    def one_more(c, carry):
        for p in range(PIECES):
            piece(table_ref, c, p, extra, xsem).start()
        span = _build_one_hot(oh_extra, 0, ridx_ref, goff_ref, c, tr)
        _patch_one_hot(oh_extra, 0, ridx_ref, c, *span, tr)
        for p in range(PIECES):
            piece(table_ref, c, p, extra, xsem).wait()
        acc[...] += _dot_tn(oh_extra[0], extra[...])
        return carry

    lax.fori_loop(COMBINE_AHEAD, nch, one_more, 0)
    xo = x_ref[0] + g2_ref[0] * acc[...]
    if final:
        xo = xo * lax.rsqrt(jnp.mean(xo * xo, axis=-1, keepdims=True) + EPS) * fg_ref[...]
    o_ref[0] = xo


def _combine(ye_flat, ridx5, table3, goff3, nrows, x, g2, fg, tt, final):
    b, t, d = x.shape
    nt = t // tt
    tr = tt // LANES
    pl_lanes = table3.shape[2]
    nxt = lambda bi, ti: jnp.minimum(bi * nt + ti + 1, b * nt - 1)
    smem = lambda width, index: pl.BlockSpec((1, 1, width), index, memory_space=pltpu.SMEM)
    grid_spec = pltpu.PrefetchScalarGridSpec(
        num_scalar_prefetch=1,
        grid=(b, nt),
        in_specs=[smem(pl_lanes, lambda bi, ti, nr: (bi * nt + ti, 0, 0)),
                  smem(pl_lanes, lambda bi, ti, nr: (nxt(bi, ti), 0, 0)),
                  smem(LANES, lambda bi, ti, nr: (bi * nt + ti, 0, 0)),
                  smem(LANES, lambda bi, ti, nr: (nxt(bi, ti), 0, 0)),
                  pl.BlockSpec((1, N_EXPERTS, 1, tr, LANES), lambda bi, ti, nr: (bi, 0, ti, 0, 0)),
                  pl.BlockSpec((1, N_EXPERTS, 1, tr, LANES),
                               lambda bi, ti, nr: (nxt(bi, ti) // nt, 0, nxt(bi, ti) % nt, 0, 0)),
                  pl.BlockSpec((1, tt, d), lambda bi, ti, nr: (bi, ti, 0)),
                  pl.BlockSpec((1, 1, d), lambda bi, ti, nr: (bi, 0, 0)),
                  pl.BlockSpec((1, d), lambda bi, ti, nr: (0, 0)),
                  pl.BlockSpec(memory_space=pl.ANY)],
        out_specs=pl.BlockSpec((1, tt, d), lambda bi, ti, nr: (bi, ti, 0)),
        scratch_shapes=[pltpu.VMEM((2, COMBINE_AHEAD, ROW_CHUNK, d), BF16),
                        pltpu.VMEM((ROW_CHUNK, d), BF16),
                        pltpu.VMEM((2, COMBINE_AHEAD, ROW_CHUNK, tt), BF16),
                        pltpu.VMEM((1, ROW_CHUNK, tt), BF16), pltpu.VMEM((tt, d), F32),
                        pltpu.SemaphoreType.DMA((2, COMBINE_AHEAD)), pltpu.SemaphoreType.DMA(())],
    )
    return pl.pallas_call(
        functools.partial(_combine_kernel, tr=tr, final=final),
        grid_spec=grid_spec,
        out_shape=jax.ShapeDtypeStruct((b, t, d), F32),
        compiler_params=_cparams("arbitrary", "arbitrary"),
    )(nrows.reshape(b * nt), table3, table3, goff3, goff3, ridx5, ridx5, x, g2, fg, ye_flat)


def _moe_tiles(t):
    tt = min(512, t)
    t_sel = max(t, 8 * LANES)
    return tt, t_sel


def _ec_moe(x, g, shift, scale, gate, wr_pad, wg, wu, wd, layer, final_g, final):
    b, t, d = x.shape
    e_n = N_EXPERTS
    cap = EC_CAPACITY_FACTOR * t // e_n
    tt, t_sel = _moe_tiles(t)
    nt, tr = t // tt, tt // LANES
    capp = cap + ROW_ALIGN * nt
    rb = min(ROW_CHUNK, capp)
    capp = -(-capp // rb) * rb
    pl_lanes = -(-(tt + ROW_ALIGN) // LANES) * LANES

    h_ext, aff_t = _router(x, g, shift, scale, wr_pad, min(512, t))
    if t_sel > t:
        aff_t = jnp.pad(aff_t, ((0, 0), (0, 0), (0, t_sel - t)), constant_values=-1.0)
    aff_r = aff_t.reshape(b, e_n, t_sel // LANES, LANES)
    ridx, table, goff, used = _select(aff_r, cap, tr, capp, pl_lanes)
    ridx5 = ridx[:, :, :nt * tr].reshape(b, e_n, nt, tr, LANES)
    table3 = table[:, :nt * tr:tr].reshape(b * nt, 1, pl_lanes)
    goff3 = goff[:, :nt * tr:tr].reshape(b * nt, 1, LANES)
    nrows2 = goff[:, :nt * tr:tr, e_n]
    used2 = used[:, :, 0, 0]

    xe = _dispatch(h_ext, ridx5, table3, goff3, nrows2, used2, capp, tt)
    ye = _ffn(xe, used2, wg, wu, wd, layer, capp, rb)
    return _combine(ye.reshape(b * e_n * capp, d), ridx5, table3, goff3, nrows2, x, gate, final_g, tt,
                    final)


def kernel(x, c, ctx, c_ctx, ada_w, ada_b, norm_g, gla_w_in, gla_w_a2, gla_b_a2, gla_norm_g,
           gla_w_out, conv_w_in, conv_k, conv_w_out, router_w, expert_w_gate, expert_w_up,
           expert_w_down, final_norm_g):
    b, t, d = x.shape
    depth = ada_w.shape[0]
    assert depth == 2 and b + 1 <= 8
    dk = gla_w_a2.shape[3]
    dv = gla_w_out.shape[1]
    hk, hv = dk // GLA_HEADS, dv // GLA_HEADS

    cvec = jnp.zeros((8, d), F32).at[:b].set(c).at[b].set(c_ctx)
    mods = _ada(cvec, ada_w, ada_b)

    def mod_vectors(layer, rows):
        m = mods[layer, rows].reshape(rows.shape[0], 1, 6, d)
        return [m[:, :, i, :] for i in range(6)]

    rows_x = jnp.arange(b)
    rows_c = jnp.full((b,), b)
    wr_hi = router_w.astype(BF16)
    wr_lo = (router_w - wr_hi.astype(F32)).astype(BF16)
    wr_pad = [jnp.pad(jnp.concatenate([wr_hi[i], wr_lo[i]], axis=1), ((0, 0), (0, LANES - 2 * N_EXPERTS)))
              for i in range(depth)]
    fg = final_norm_g.reshape(1, d)

    sh1x, sc1x, g1x, sh2x, sc2x, g2x = mod_vectors(0, rows_x)
    sh1c, sc1c, g1c, sh2c, sc2c, g2c = mod_vectors(0, rows_c)
    n_main = 2 * dk + 2 * dv
    w_main = jnp.pad(gla_w_in[0], ((0, 0), (0, LANES - 2 * GLA_RANK))).astype(BF16)
    assert w_main.shape[1] == n_main + LANES
    w2 = jnp.zeros((2 * GLA_RANK, 2 * dk), F32)
    w2 = w2.at[:GLA_RANK, :dk].set(gla_w_a2[0, 0]).at[GLA_RANK:, dk:].set(gla_w_a2[0, 1])
    w2_hi = w2.astype(BF16)
    w2_lo = (w2 - w2_hi.astype(F32)).astype(BF16)
    w2 = jnp.concatenate([w2_hi, w2_hi, w2_lo, jnp.zeros((LANES - 6 * GLA_RANK, 2 * dk), BF16)])
    b2 = gla_b_a2[0].reshape(1, 2 * dk)
    gn = gla_norm_g[0].reshape(1, hv)
    w_out = gla_w_out[0].astype(BF16)
    g_n1 = norm_g[0, 0].reshape(1, d)
    g_n2 = norm_g[0, 1].reshape(1, d)

    tc = ctx.shape[1]
    qc, kc, vc, rc, lac = _gla_in(ctx, g_n1, sh1c, sc1c, w_main, w2, b2, min(512, tc))
    s0 = jnp.zeros((b, GLA_HEADS, hv, hk), F32)
    ocf, ocb, s_cf, s_cb = _gla_scan(qc, kc, vc, lac, s0, s0)
    qx, kx, vx, rx, lax_ = _gla_in(x, g_n1, sh1x, sc1x, w_main, w2, b2, 512)
    oxf, oxb, _, _, wg, wu, wd = _gla_scan(qx, kx, vx, lax_, s_cf, s_cb,
                                           cast=(expert_w_gate, expert_w_up, expert_w_down))
    wg, wu, wd = (w.reshape(s.shape) for w, s in
                  zip((wg, wu, wd), (expert_w_gate, expert_w_up, expert_w_down)))
    x = _gla_out(oxf, oxb, rx, x, gn, w_out, g1x, 512)
    x = _ec_moe(x, g_n2, sh2x, sc2x, g2x, wr_pad[0], wg, wu, wd, 0, fg, False)

    ctx = _gla_out(ocf, ocb, rc, ctx, gn, w_out, g1c, min(512, tc))
    ctx = _ec_moe(ctx, g_n2, sh2c, sc2c, g2c, wr_pad[0], wg, wu, wd, 0, fg, False)
    del ctx

    sh1x, sc1x, g1x, sh2x, sc2x, g2x = mod_vectors(1, rows_x)
    x = _conv_mix(x, norm_g[1, 0].reshape(1, d), sh1x, sc1x, conv_w_in[0].astype(BF16), conv_k[0],
                  conv_w_out[0].astype(BF16), g1x, GRID_W, 512)
    return _ec_moe(x, norm_g[1, 1].reshape(1, d), sh2x, sc2x, g2x, wr_pad[1], wg, wu, wd, 1, fg, True)
```

```python
import functools

import jax
import jax.numpy as jnp
from jax import lax
from jax.experimental import pallas as pl
from jax.experimental.pallas import tpu as pltpu

F32 = jnp.float32
BF16 = jnp.bfloat16
I32 = jnp.int32
HIGHEST = lax.Precision.HIGHEST

EPS = 1e-6
GRID_W = 64
GLA_HEADS = 4
GLA_RANK = 16
GLA_TAU = 16.0
N_EXPERTS = 16
EC_CAPACITY_FACTOR = 2
CONV_WIDTH = 3

LANES = 128
ROW_ALIGN = 16
GLA_CHUNK = 256
GLA_DIAG = 8
ROW_CHUNK = 256
PIECES = ROW_CHUNK // ROW_ALIGN
ONE_HOT_SLOTS = 3
DISPATCH_CHUNKS = 5
TRASH_ROWS = (2 * DISPATCH_CHUNKS + 1) * ROW_CHUNK
COMBINE_AHEAD = 5
CAST_GROUP = 4
MIN_NORMAL_BITS = 0x00800000
REFINE_STEPS = 40
VMEM_LIMIT = 60 * 1024 * 1024


def _cparams(*sem):
    return pltpu.CompilerParams(dimension_semantics=sem, vmem_limit_bytes=VMEM_LIMIT)


def _dot(a, b):
    return jnp.dot(a, b, preferred_element_type=F32)


def _dot_nt(a, b):
    return lax.dot_general(a, b, (((1,), (1,)), ((), ())), preferred_element_type=F32)


def _dot_tn(a, b):
    return lax.dot_general(a, b, (((0,), (0,)), ((), ())), preferred_element_type=F32)


def _xdot(a, b):
    return jnp.dot(a, b, precision=HIGHEST, preferred_element_type=F32)


def _split2(x):
    hi = x.astype(BF16)
    return hi, (x - hi.astype(F32)).astype(BF16)


def _silu(x):
    return x * jax.nn.sigmoid(x)


def _norm_mod(x, g, shift, scale):
    y = x * lax.rsqrt(jnp.mean(x * x, axis=-1, keepdims=True) + EPS) * g
    return y * (1.0 + scale) + shift


def _split3(x):
    hi = x.astype(BF16)
    r1 = x - hi.astype(F32)
    mid = r1.astype(BF16)
    lo = (r1 - mid.astype(F32)).astype(BF16)
    return hi, mid, lo


def _ada_kernel(c_ref, w_ref, b_ref, o_ref):
    o_ref[0] = _xdot(_silu(c_ref[...]), w_ref[0]) + b_ref[0]


def _ada(cvec, ada_w, ada_b):
    depth, d, n6 = ada_w.shape
    nt = n6 // d
    return pl.pallas_call(
        _ada_kernel,
        grid=(depth, nt),
        in_specs=[pl.BlockSpec((8, d), lambda l, n: (0, 0)),
                  pl.BlockSpec((1, d, d), lambda l, n: (l, 0, n)),
                  pl.BlockSpec((1, 1, d), lambda l, n: (l, 0, n))],
        out_specs=pl.BlockSpec((1, 8, d), lambda l, n: (l, 0, n)),
        out_shape=jax.ShapeDtypeStruct((depth, 8, n6), F32),
        compiler_params=_cparams("arbitrary", "arbitrary"),
    )(cvec, ada_w, ada_b.reshape(depth, 1, n6))


def _gla_in_kernel(x_ref, g_ref, sh_ref, sc_ref, w_ref, w2_ref, b2_ref,
                   q_ref, k_ref, v_ref, r_ref, la_ref, *, dk, dv):
    h = _norm_mod(x_ref[0], g_ref[...], sh_ref[0], sc_ref[0])
    y = _dot(h.astype(BF16), w_ref[...])
    hk = dk // GLA_HEADS
    q_ref[0] = y[:, :dk] * (hk ** -0.5)
    k_ref[0] = y[:, dk:2 * dk]
    v_ref[0] = y[:, 2 * dk:2 * dk + dv]
    r_ref[0] = y[:, 2 * dk + dv:2 * dk + 2 * dv].astype(BF16)
    a_hi, a_lo = _split2(y[:, 2 * dk + 2 * dv:])
    a3 = (a_hi.astype(F32) + pltpu.roll(a_lo.astype(F32), 2 * GLA_RANK, 1)
          + pltpu.roll(a_hi.astype(F32), 4 * GLA_RANK, 1))
    z = _dot(a3.astype(BF16), w2_ref[...]) + b2_ref[...]
    log_sig = jnp.minimum(z, 0.0) - jnp.log(1.0 + jnp.exp(-jnp.abs(z)))
    la_ref[0] = log_sig * (1.0 / GLA_TAU)


def _gla_in(x, g, shift, scale, w_main, w2, b2, tm):
    b, t, d = x.shape
    dk = w2.shape[1] // 2
    dv = (w_main.shape[1] - LANES - 2 * dk) // 2
    nw = w_main.shape[1]
    tok = lambda width: pl.BlockSpec((1, tm, width), lambda bi, ti: (bi, ti, 0))
    vec = pl.BlockSpec((1, 1, d), lambda bi, ti: (bi, 0, 0))
    full = lambda shape: pl.BlockSpec(shape, lambda bi, ti: tuple(0 for _ in shape))
    return pl.pallas_call(
        functools.partial(_gla_in_kernel, dk=dk, dv=dv),
        grid=(b, t // tm),
        in_specs=[tok(d), full((1, d)), vec, vec, full((d, nw)), full((LANES, 2 * dk)), full((1, 2 * dk))],
        out_specs=[tok(dk), tok(dk), tok(dv), tok(dv), tok(2 * dk)],
        out_shape=[jax.ShapeDtypeStruct((b, t, dk), F32), jax.ShapeDtypeStruct((b, t, dk), F32),
                   jax.ShapeDtypeStruct((b, t, dv), F32), jax.ShapeDtypeStruct((b, t, dv), BF16),
                   jax.ShapeDtypeStruct((b, t, 2 * dk), F32)],
        compiler_params=_cparams("arbitrary", "arbitrary"),
    )(x, g, shift, scale, w_main, w2, b2)


def _gla_chunk(q, k, v, la, st, rev):
    c, hk = q.shape
    ii = lax.broadcasted_iota(I32, (c, c), 0)
    jj = lax.broadcasted_iota(I32, (c, c), 1)
    tri = ((jj >= ii) if rev else (jj <= ii)).astype(BF16)
    hi, mid, lo = _split3(la)
    acum = _dot(tri, hi) + _dot(tri, mid) + _dot(tri, lo)
    a_last = acum[0:1] if rev else acum[c - 1:c]

    o = _dot_nt((q * jnp.exp(acum)).astype(BF16), st.astype(BF16))

    xor = ii ^ jj
    rowi = lax.broadcasted_iota(I32, (c, hk), 0)
    p = jnp.zeros((c, c), F32)
    blk = c
    while blk >= 2 * GLA_DIAG:
        half = blk // 2
        a3 = acum.reshape(c // blk, blk, hk)
        ridx = half if rev else half - 1
        ref = jnp.broadcast_to(a3[:, ridx:ridx + 1, :], a3.shape).reshape(c, hk)
        pos = rowi & (blk - 1)
        late = (pos < half) if rev else (pos >= half)
        qe = jnp.where(late, q * jnp.exp(acum - ref), 0.0).astype(BF16)
        ke = jnp.where(late, 0.0, k * jnp.exp(ref - acum)).astype(BF16)
        raw = _dot_nt(qe, ke)
        p = p + (raw if blk == c else jnp.where(xor < blk, raw, 0.0))
        blk = half

    nb = c // GLA_DIAG
    q8 = q.reshape(nb, GLA_DIAG, hk)
    k8 = k.reshape(nb, GLA_DIAG, hk)
    a8 = acum.reshape(nb, GLA_DIAG, hk)
    sub = lax.broadcasted_iota(I32, (nb, GLA_DIAG, hk), 1)
    colmod = jj & (GLA_DIAG - 1)
    pd = jnp.zeros((c, c), F32)
    for j in range(GLA_DIAG):
        kj = jnp.broadcast_to(k8[:, j:j + 1, :], k8.shape)
        aj = jnp.broadcast_to(a8[:, j:j + 1, :], a8.shape)
        valid = (sub <= j) if rev else (sub >= j)
        dec = jnp.where(valid, jnp.exp(a8 - aj), 0.0)
        cj = jnp.sum(q8 * kj * dec, axis=2, keepdims=True).reshape(c, 1)
        pd = jnp.where(colmod == j, cj, pd)
    p = p + jnp.where(xor < GLA_DIAG, pd, 0.0)

    vb = v.astype(BF16)
    o = o + _dot(p.astype(BF16), vb)
    kd = (k * jnp.exp(a_last - acum)).astype(BF16)
    st_new = st * jnp.exp(a_last) + _dot_tn(vb, kd)
    return o, st_new


def _gla_scan_kernel(*refs, n_cast):
    (qf_ref, kf_ref, vf_ref, laf_ref, qb_ref, kb_ref, vb_ref, lab_ref, s0f_ref, s0b_ref) = refs[:10]
    w_in = refs[10:10 + n_cast]
    of_ref, ob_ref, sf_ref, sb_ref = refs[10 + n_cast:14 + n_cast]
    w_out = refs[14 + n_cast:14 + 2 * n_cast]
    stf, stb = refs[14 + 2 * n_cast:]
    n = pl.program_id(2)
    step = (pl.program_id(0) * pl.num_programs(1) + pl.program_id(1)) * pl.num_programs(2) + n

    @pl.when(step % CAST_GROUP == 0)
    def _():
        for wi, wo in zip(w_in, w_out):
            wo[0] = wi[0, 0].astype(BF16)

    @pl.when(n == 0)
    def _():
        stf[...] = s0f_ref[0, 0]
        stb[...] = s0b_ref[0, 0]

    o, s = _gla_chunk(qf_ref[0], kf_ref[0], vf_ref[0], laf_ref[0], stf[...], False)
    of_ref[0] = o.astype(BF16)
    stf[...] = s
    o, s = _gla_chunk(qb_ref[0], kb_ref[0], vb_ref[0], lab_ref[0], stb[...], True)
    ob_ref[0] = o.astype(BF16)
    stb[...] = s

    @pl.when(n == pl.num_programs(2) - 1)
    def _():
        sf_ref[0, 0] = stf[...]
        sb_ref[0, 0] = stb[...]


def _gla_scan(q, k, v, la, s0f, s0b, cast=()):
    b, t, dk = q.shape
    dv = v.shape[2]
    hk, hv = dk // GLA_HEADS, dv // GLA_HEADS
    c = GLA_CHUNK
    nc = t // c
    steps = b * GLA_HEADS * nc
    fwd = lambda w, off: pl.BlockSpec((1, c, w), lambda bi, h, n: (bi, n, h + off))
    bwd = lambda w, off: pl.BlockSpec((1, c, w), lambda bi, h, n: (bi, nc - 1 - n, h + off))
    st = pl.BlockSpec((1, 1, hv, hk), lambda bi, h, n: (bi, h, 0, 0))
    cast_in, cast_out, cast_shape = [], [], []
    for w in cast:
        n_l, n_e, rows, cols = w.shape
        groups = steps // CAST_GROUP
        per = groups // (n_l * n_e)
        rb = rows // per
        assert (groups * CAST_GROUP == steps and per * n_l * n_e == groups and rb * per == rows
                and rb % ROW_ALIGN == 0)
        step = lambda bi, h, n: ((bi * GLA_HEADS + h) * nc + n) // CAST_GROUP
        cast_in.append(pl.BlockSpec(
            (1, 1, rb, cols),
            lambda bi, h, n, per=per, n_e=n_e: (step(bi, h, n) // per // n_e, step(bi, h, n) // per % n_e,
                                                step(bi, h, n) % per, 0)))
        cast_out.append(pl.BlockSpec(
            (1, rb, cols), lambda bi, h, n, per=per: (step(bi, h, n) // per, step(bi, h, n) % per, 0)))
        cast_shape.append(jax.ShapeDtypeStruct((n_l * n_e, rows, cols), BF16))
    return pl.pallas_call(
        functools.partial(_gla_scan_kernel, n_cast=len(cast)),
        grid=(b, GLA_HEADS, nc),
        in_specs=[fwd(hk, 0), fwd(hk, 0), fwd(hv, 0), fwd(hk, 0),
                  bwd(hk, 0), bwd(hk, 0), bwd(hv, 0), bwd(hk, GLA_HEADS), st, st] + cast_in,
        out_specs=[fwd(hv, 0), bwd(hv, 0), st, st] + cast_out,
        out_shape=[jax.ShapeDtypeStruct((b, t, dv), BF16), jax.ShapeDtypeStruct((b, t, dv), BF16),
                   jax.ShapeDtypeStruct((b, GLA_HEADS, hv, hk), F32),
                   jax.ShapeDtypeStruct((b, GLA_HEADS, hv, hk), F32)] + cast_shape,
        scratch_shapes=[pltpu.VMEM((hv, hk), F32), pltpu.VMEM((hv, hk), F32)],
        compiler_params=_cparams("arbitrary", "arbitrary", "arbitrary"),
    )(q, k, v, la, q, k, v, la, s0f, s0b, *cast)


def _gla_out_kernel(of_ref, ob_ref, r_ref, x_ref, gn_ref, w_ref, g1_ref, o_ref):
    o = of_ref[0].astype(F32) + ob_ref[0].astype(F32)
    hv = gn_ref.shape[1]
    parts = []
    for h in range(GLA_HEADS):
        oh = o[:, h * hv:(h + 1) * hv]
        ms = jnp.mean(oh * oh, axis=-1, keepdims=True)
        parts.append(oh * lax.rsqrt(ms + EPS) * gn_ref[...])
    y = jnp.concatenate(parts, axis=1) * _silu(r_ref[0].astype(F32))
    o_ref[0] = x_ref[0] + g1_ref[0] * _dot(y.astype(BF16), w_ref[...])


def _gla_out(o_f, o_b, r, x, gn, w_out, g1, tm):
    b, t, d = x.shape
    dv = o_f.shape[2]
    tok = lambda width: pl.BlockSpec((1, tm, width), lambda bi, ti: (bi, ti, 0))
    return pl.pallas_call(
        _gla_out_kernel,
        grid=(b, t // tm),
        in_specs=[tok(dv), tok(dv), tok(dv), tok(d),
                  pl.BlockSpec((1, dv // GLA_HEADS), lambda bi, ti: (0, 0)),
                  pl.BlockSpec((dv, d), lambda bi, ti: (0, 0)),
                  pl.BlockSpec((1, 1, d), lambda bi, ti: (bi, 0, 0))],
        out_specs=tok(d),
        out_shape=jax.ShapeDtypeStruct((b, t, d), F32),
        compiler_params=_cparams("arbitrary", "arbitrary"),
    )(o_f, o_b, r, x, gn, w_out, g1)


def _conv_mix_kernel(x_ref, g_ref, sh_ref, sc_ref, win_ref, ck_ref, wout_ref, g1_ref, o_ref, *, seg):
    x = x_ref[0]
    tm, d = x.shape
    h = _norm_mod(x, g_ref[...], sh_ref[0], sc_ref[0])
    y = _dot(h.astype(BF16), win_ref[...])
    bg, cg, v = y[:, :d], y[:, d:2 * d], y[:, 2 * d:]
    u = cg * v
    pos = lax.broadcasted_iota(I32, (tm, 1), 0) % seg
    u_prev = jnp.where(pos == 0, 0.0, pltpu.roll(u, 1, 0))
    u_next = jnp.where(pos == seg - 1, 0.0, pltpu.roll(u, tm - 1, 0))
    conv = u_prev * ck_ref[0:1, :] + u * ck_ref[1:2, :] + u_next * ck_ref[2:3, :]
    o_ref[0] = x + g1_ref[0] * _dot((bg * conv).astype(BF16), wout_ref[...])


def _conv_mix(x, g, shift, scale, w_in, ck, w_out, g1, seg, tm):
    b, t, d = x.shape
    tok = pl.BlockSpec((1, tm, d), lambda bi, ti: (bi, ti, 0))
    vec = pl.BlockSpec((1, 1, d), lambda bi, ti: (bi, 0, 0))
    full = lambda shape: pl.BlockSpec(shape, lambda bi, ti: tuple(0 for _ in shape))
    return pl.pallas_call(
        functools.partial(_conv_mix_kernel, seg=seg),
        grid=(b, t // tm),
        in_specs=[tok, full((1, d)), vec, vec, full((d, 3 * d)), full((CONV_WIDTH, d)), full((d, d)), vec],
        out_specs=tok,
        out_shape=jax.ShapeDtypeStruct((b, t, d), F32),
        compiler_params=_cparams("arbitrary", "arbitrary"),
    )(x, g, shift, scale, w_in, ck, w_out, g1)


def _router_kernel(x_ref, g_ref, sh_ref, sc_ref, wr_ref, h_ref, aff_ref):
    d = x_ref.shape[2]
    h = _norm_mod(x_ref[0], g_ref[...], sh_ref[0], sc_ref[0])
    h_hi, h_lo = _split2(h)
    p = _dot(h_hi, wr_ref[...]) + _dot(h_lo, wr_ref[...])
    logits = p + pltpu.roll(p, LANES - N_EXPERTS, 1)
    lane = lax.broadcasted_iota(I32, logits.shape, 1)
    is_e = lane < N_EXPERTS
    m = jnp.max(jnp.where(is_e, logits, -jnp.inf), axis=-1, keepdims=True)
    ex = jnp.where(is_e, jnp.exp(logits - m), 0.0)
    aff = ex / jnp.sum(ex, axis=-1, keepdims=True)
    aff_ref[0] = aff.T[:N_EXPERTS, :]
    hi, mid, lo = _split3(aff)
    ext = (hi.astype(F32) + pltpu.roll(mid.astype(F32), N_EXPERTS, 1)
           + pltpu.roll(lo.astype(F32), 2 * N_EXPERTS, 1))
    h_ref[0, :, :d] = h.astype(BF16)
    h_ref[0, :, d:] = ext.astype(BF16)


def _router(x, g, shift, scale, wr_pad, tm):
    b, t, d = x.shape
    tok = lambda width: pl.BlockSpec((1, tm, width), lambda bi, ti: (bi, ti, 0))
    vec = pl.BlockSpec((1, 1, d), lambda bi, ti: (bi, 0, 0))
    return pl.pallas_call(
        _router_kernel,
        grid=(b, t // tm),
        in_specs=[tok(d), pl.BlockSpec((1, d), lambda bi, ti: (0, 0)), vec, vec,
                  pl.BlockSpec((d, LANES), lambda bi, ti: (0, 0))],
        out_specs=[tok(d + LANES), pl.BlockSpec((1, N_EXPERTS, tm), lambda bi, ti: (bi, 0, ti))],
        out_shape=[jax.ShapeDtypeStruct((b, t, d + LANES), BF16),
                   jax.ShapeDtypeStruct((b, N_EXPERTS, t), F32)],
        compiler_params=_cparams("arbitrary", "arbitrary"),
    )(x, g, shift, scale, wr_pad)


def _select_kernel(aff_ref, ridx_ref, table_ref, goff_ref, used_ref, *, cap, tr, capp, pl_lanes):
    e_n = N_EXPERTS
    a = aff_ref[0]
    r_n = a.shape[1]

    def count_ge(v):
        return jnp.sum(jnp.where(a >= v, 1.0, 0.0), axis=(1, 2), keepdims=True)

    def search(it, cur):
        cand = cur | (jnp.int32(1) << (30 - it))
        return jnp.where(count_ge(lax.bitcast_convert_type(cand, F32)) >= cap, cand, cur)

    thr = lax.fori_loop(0, 31, search, jnp.zeros((e_n, 1, 1), I32))
    normal = thr >= MIN_NORMAL_BITS
    lo = lax.bitcast_convert_type(jnp.where(normal, thr, 0), F32)
    hi = lax.bitcast_convert_type(jnp.where(normal, thr + 1, MIN_NORMAL_BITS), F32)

    def refine(it, lh):
        lo_, hi_ = lh
        mid = 0.5 * (lo_ + hi_)
        ok = count_ge(mid) >= cap
        return jnp.where(ok, mid, lo_), jnp.where(ok, hi_, mid)

    lo, hi = lax.fori_loop(0, REFINE_STEPS, refine, (lo, hi))
    gt = a >= hi
    eq = (a >= lo) & jnp.logical_not(gt)
    need = cap - jnp.sum(jnp.where(gt, 1.0, 0.0), axis=(1, 2), keepdims=True)

    li = lax.broadcasted_iota(I32, (LANES, LANES), 0)
    lj = lax.broadcasted_iota(I32, (LANES, LANES), 1)
    upper = (li <= lj).astype(BF16)
    ones = jnp.ones((LANES, LANES), BF16)
    ri = lax.broadcasted_iota(I32, (r_n, r_n), 0)
    rj = lax.broadcasted_iota(I32, (r_n, r_n), 1)
    strict_lower = (rj < ri).astype(BF16)

    def cumsum_tokens(mask):
        m = jnp.where(mask, 1.0, 0.0).astype(BF16).reshape(e_n * r_n, LANES)
        within = _dot(m, upper).reshape(e_n, r_n, LANES)
        rs = _dot(m, ones).reshape(e_n, r_n, LANES)
        pre = jnp.stack([_dot(strict_lower, rs[e].astype(BF16)) for e in range(e_n)])
        return within + pre, rs

    eq_rank, _ = cumsum_tokens(eq)
    sel = gt | (eq & (eq_rank <= need))
    incl, rs = cumsum_tokens(sel)
    pos = incl - jnp.where(sel, 1.0, 0.0)

    same_tile = ((ri // tr) == (rj // tr)).astype(F32)
    prev_tile = ((rj // tr) < (ri // tr)).astype(F32)
    first_row = (rj % tr) == 0
    prev_first = jnp.where(first_row, prev_tile, 0.0)
    all_first = jnp.where(first_row, 1.0, 0.0)

    goff = jnp.zeros((r_n, LANES), F32)
    lane = lax.broadcasted_iota(I32, (r_n, LANES), 1)
    gtab = jnp.zeros((r_n, LANES), F32)
    rows = []
    tbl = jnp.full((r_n, pl_lanes), -1.0, F32)
    row16 = (lax.broadcasted_iota(I32, (r_n, pl_lanes), 1) * ROW_ALIGN).astype(F32)
    rep = lambda x: jnp.concatenate([x] * (pl_lanes // LANES), axis=1)
    base = (pl.program_id(0) * e_n * capp).astype(F32)
    for e in range(e_n):
        n_tile = _xdot(same_tile, rs[e])
        pos_start = _xdot(prev_tile, rs[e])
        n_pad = jnp.floor((n_tile + (ROW_ALIGN - 1)) * (1.0 / ROW_ALIGN)) * ROW_ALIGN
        off = _xdot(prev_first, n_pad)
        used_ref[0, e] = _xdot(all_first, n_pad).astype(I32)
        rows.append(jnp.where(sel[e], goff + pos[e] - pos_start, -1.0))
        gtab = jnp.where(lane == e, goff, gtab)
        g0, g1 = rep(goff), rep(goff + n_pad)
        inside = (row16 >= g0) & (row16 < g1)
        tbl = jnp.where(inside, base + e * capp + rep(off) + row16 - g0, tbl)
        goff = goff + n_pad
    ridx_ref[0] = jnp.stack(rows).astype(I32)
    table_ref[0] = tbl.astype(I32)
    goff_ref[0] = jnp.where(lane >= e_n, goff, gtab).astype(I32)


def _select(aff_r, cap, tr, capp, pl_lanes):
    b, e_n, r_n, _ = aff_r.shape
    blk4 = pl.BlockSpec((1, e_n, r_n, LANES), lambda bi: (bi, 0, 0, 0))
    return pl.pallas_call(
        functools.partial(_select_kernel, cap=cap, tr=tr, capp=capp, pl_lanes=pl_lanes),
        grid=(b,),
        in_specs=[blk4],
        out_specs=[blk4, pl.BlockSpec((1, r_n, pl_lanes), lambda bi: (bi, 0, 0)),
                   pl.BlockSpec((1, r_n, LANES), lambda bi: (bi, 0, 0)), blk4],
        out_shape=[jax.ShapeDtypeStruct((b, e_n, r_n, LANES), I32),
                   jax.ShapeDtypeStruct((b, r_n, pl_lanes), I32),
                   jax.ShapeDtypeStruct((b, r_n, LANES), I32),
                   jax.ShapeDtypeStruct((b, e_n, r_n, LANES), I32)],
        compiler_params=_cparams("arbitrary"),
    )(aff_r)


def _expert_span(goff_ref, base):
    e_lo = jnp.int32(0)
    e_hi = jnp.int32(0)
    for e in range(N_EXPERTS):
        e_lo += (goff_ref[0, 0, e + 1] <= base).astype(I32)
        e_hi += (goff_ref[0, 0, e] < base + ROW_CHUNK).astype(I32)
    return e_lo, e_hi


def _one_hot_rows(ridx_ref, base, e_lo, e_hi, tr):
    riota = lax.broadcasted_iota(I32, (ROW_CHUNK, LANES), 0) + base
    pieces = []
    for q in range(tr):
        acc = jnp.zeros((ROW_CHUNK, LANES), F32)
        for k in range(ONE_HOT_SLOTS):
            e = jnp.minimum(e_lo + k, N_EXPERTS - 1)
            row = jnp.where(e_lo + k < e_hi, ridx_ref[0, e, 0, pl.ds(q, 1), :], -2)
            acc = jnp.where(row == riota, 1.0, acc)
        pieces.append(acc.astype(BF16))
    return jnp.concatenate(pieces, axis=1)


def _one_hot_rows_any(ridx_ref, base, e_lo, e_hi, tr):
    riota = lax.broadcasted_iota(I32, (ROW_CHUNK, LANES), 0) + base
    pieces = []
    for q in range(tr):
        def add_expert(e, acc, q=q):
            return jnp.where(ridx_ref[0, e, 0, pl.ds(q, 1), :] == riota, 1.0, acc)
        acc = lax.fori_loop(e_lo, e_hi, add_expert, jnp.zeros((ROW_CHUNK, LANES), F32))
        pieces.append(acc.astype(BF16))
    return jnp.concatenate(pieces, axis=1)


def _build_one_hot(oh_ref, slot, ridx_ref, goff_ref, c, tr):
    base = c * ROW_CHUNK
    e_lo, e_hi = _expert_span(goff_ref, base)
    oh_ref[slot] = _one_hot_rows(ridx_ref, base, e_lo, e_hi, tr)
    return e_lo, e_hi


def _patch_one_hot(oh_ref, slot, ridx_ref, c, e_lo, e_hi, tr):
    @pl.when(e_hi - e_lo > ONE_HOT_SLOTS)
    def _():
        oh_ref[slot] = _one_hot_rows_any(ridx_ref, c * ROW_CHUNK, e_lo, e_hi, tr)


def _dispatch_kernel(nrows_ref, used_ref, table_ref, goff_ref, next_goff_ref, x_ref, ridx_ref,
                     next_ridx_ref, xe_ref, stage, extra, oh, oh_extra, zeros, sem, xsem, zsem,
                     *, tr, capp):
    bi, ti = pl.program_id(0), pl.program_id(1)
    step = bi * pl.num_programs(1) + ti
    n_steps = pl.num_programs(0) * pl.num_programs(1)
    par = step % 2
    nch = (nrows_ref[bi, ti] + (ROW_CHUNK - 1)) // ROW_CHUNK
    x = x_ref[0]
    trash = pl.num_programs(0) * N_EXPERTS * capp
    n_big = DISPATCH_CHUNKS * PIECES

    def piece(src, j, dst, dsem):
        return pltpu.make_async_copy(
            src.at[pl.ds(j * ROW_ALIGN, ROW_ALIGN), :],
            xe_ref.at[pl.ds(pl.multiple_of(dst, ROW_ALIGN), ROW_ALIGN), :], dsem)

    def wait_big(parity):
        for j in range(n_big):
            piece(stage.at[parity], j, 0, sem.at[parity]).wait()

    def build_tile(rref, gref, parity):
        return [_build_one_hot(oh.at[parity], c, rref, gref, c, tr) for c in range(DISPATCH_CHUNKS)]

    def patch_tile(rref, parity, spans):
        for c, span in enumerate(spans):
            _patch_one_hot(oh.at[parity], c, rref, c, *span, tr)

    pl.when(step == 0)(lambda: patch_tile(ridx_ref, 0, build_tile(ridx_ref, goff_ref, 0)))
    pl.when(step >= 2)(lambda: wait_big(par))

    next_spans = build_tile(next_ridx_ref, next_goff_ref, 1 - par)
    rows = DISPATCH_CHUNKS * ROW_CHUNK
    stage[par] = _dot(oh[par].reshape(rows, oh.shape[3]), x).astype(BF16)
    for j in range(n_big):
        dst = table_ref[0, 0, j]
        dst = jnp.where(dst >= 0, dst, trash + (par * n_big + j) * ROW_ALIGN)
        piece(stage.at[par], j, dst, sem.at[par]).start()
    patch_tile(next_ridx_ref, 1 - par, next_spans)

    def one_more(c, carry):
        span = _build_one_hot(oh_extra, 0, ridx_ref, goff_ref, c, tr)
        _patch_one_hot(oh_extra, 0, ridx_ref, c, *span, tr)
        extra[...] = _dot(oh_extra[0], x).astype(BF16)
        for p in range(PIECES):
            dst = table_ref[0, 0, c * PIECES + p]
            dst = jnp.where(dst >= 0, dst, trash + (2 * n_big + p) * ROW_ALIGN)
            piece(extra, p, dst, xsem).start()
        for p in range(PIECES):
            piece(extra, p, 0, xsem).wait()
        return carry

    lax.fori_loop(DISPATCH_CHUNKS, nch, one_more, 0)

    @pl.when(step == n_steps - 1)
    def _():
        wait_big(par)
        pl.when(n_steps >= 2)(lambda: wait_big(1 - par))

    def zero_piece(row):
        return pltpu.make_async_copy(
            zeros, xe_ref.at[pl.ds(pl.multiple_of(row, ROW_ALIGN), ROW_ALIGN), :], zsem)

    def zero_rows(first, n_pieces):
        def start(k, carry):
            zero_piece(first + k * ROW_ALIGN).start()
            return carry

        def wait(k, carry):
            zero_piece(first + k * ROW_ALIGN).wait()
            return carry

        lax.fori_loop(0, n_pieces, start, 0)
        lax.fori_loop(0, n_pieces, wait, 0)

    @pl.when(ti == pl.num_programs(1) - 1)
    def _():
        zeros[...] = jnp.zeros(zeros.shape, BF16)
        for e in range(N_EXPERTS):
            used = used_ref[bi, e]
            zero_rows((bi * N_EXPERTS + e) * capp + used, (capp - used) // ROW_ALIGN)

        @pl.when(bi == pl.num_programs(0) - 1)
        def _():
            zero_rows(trash, TRASH_ROWS // ROW_ALIGN)


def _dispatch(h_ext, ridx5, table3, goff3, nrows, used, capp, tt):
    b, t, dx = h_ext.shape
    nt = t // tt
    tr = tt // LANES
    pl_lanes = table3.shape[2]
    nxt = lambda bi, ti: jnp.minimum(bi * nt + ti + 1, b * nt - 1)
    grid_spec = pltpu.PrefetchScalarGridSpec(
        num_scalar_prefetch=2,
        grid=(b, nt),
        in_specs=[pl.BlockSpec((1, 1, pl_lanes), lambda bi, ti, nr, us: (bi * nt + ti, 0, 0),
                               memory_space=pltpu.SMEM),
                  pl.BlockSpec((1, 1, LANES), lambda bi, ti, nr, us: (bi * nt + ti, 0, 0),
                               memory_space=pltpu.SMEM),
                  pl.BlockSpec((1, 1, LANES), lambda bi, ti, nr, us: (nxt(bi, ti), 0, 0),
                               memory_space=pltpu.SMEM),
                  pl.BlockSpec((1, tt, dx), lambda bi, ti, nr, us: (bi, ti, 0)),
                  pl.BlockSpec((1, N_EXPERTS, 1, tr, LANES), lambda bi, ti, nr, us: (bi, 0, ti, 0, 0)),
                  pl.BlockSpec((1, N_EXPERTS, 1, tr, LANES),
                               lambda bi, ti, nr, us: (nxt(bi, ti) // nt, 0, nxt(bi, ti) % nt, 0, 0))],
        out_specs=pl.BlockSpec(memory_space=pl.ANY),
        scratch_shapes=[pltpu.VMEM((2, DISPATCH_CHUNKS * ROW_CHUNK, dx), BF16),
                        pltpu.VMEM((ROW_CHUNK, dx), BF16),
                        pltpu.VMEM((2, DISPATCH_CHUNKS, ROW_CHUNK, tt), BF16),
                        pltpu.VMEM((1, ROW_CHUNK, tt), BF16), pltpu.VMEM((ROW_ALIGN, dx), BF16),
                        pltpu.SemaphoreType.DMA((2,)), pltpu.SemaphoreType.DMA(()),
                        pltpu.SemaphoreType.DMA(())],
    )
    return pl.pallas_call(
        functools.partial(_dispatch_kernel, tr=tr, capp=capp),
        grid_spec=grid_spec,
        out_shape=jax.ShapeDtypeStruct((b * N_EXPERTS * capp + TRASH_ROWS, dx), BF16),
        compiler_params=_cparams("arbitrary", "arbitrary"),
    )(nrows, used, table3, goff3, goff3, h_ext, ridx5, ridx5)


def _ffn_kernel(used_ref, x_ref, wg_ref, wu_ref, wd_ref, o_ref, *, rb):
    e, bi = pl.program_id(0), pl.program_id(1)
    capp, d = o_ref.shape[2], o_ref.shape[3]
    nblk = (used_ref[bi, e] + (rb - 1)) // rb
    lane = lax.broadcasted_iota(I32, (1, LANES), 1)
    own = (lane == e) | (lane == e + N_EXPERTS) | (lane == e + 2 * N_EXPERTS)

    def body(j, carry):
        rows = pl.ds(pl.multiple_of(j * rb, rb), rb)
        xb = x_ref[rows, :]
        xm = xb[:, :d]
        val = jnp.sum(jnp.where(own, xb[:, d:].astype(F32), 0.0), axis=1, keepdims=True)
        hid = (_silu(_dot(xm, wg_ref[0, 0])) * _dot(xm, wu_ref[0, 0])).astype(BF16)
        o_ref[0, 0, rows, :] = (_dot(hid, wd_ref[0, 0]) * val).astype(BF16)
        return carry

    def zero(j, carry):
        o_ref[0, 0, pl.ds(pl.multiple_of(j * rb, rb), rb), :] = jnp.zeros((rb, d), BF16)
        return carry

    lax.fori_loop(0, nblk, body, 0)
    lax.fori_loop(nblk, capp // rb, zero, 0)


def _ffn(xe, used, wg, wu, wd, layer, capp, rb):
    b, e_n = used.shape
    dx = xe.shape[1]
    d, f = wg.shape[2], wg.shape[3]
    grid_spec = pltpu.PrefetchScalarGridSpec(
        num_scalar_prefetch=1,
        grid=(e_n, b),
        in_specs=[pl.BlockSpec((capp, dx), lambda e, bi, u: (bi * e_n + e, 0)),
                  pl.BlockSpec((1, 1, d, f), lambda e, bi, u: (layer, e, 0, 0)),
                  pl.BlockSpec((1, 1, d, f), lambda e, bi, u: (layer, e, 0, 0)),
                  pl.BlockSpec((1, 1, f, d), lambda e, bi, u: (layer, e, 0, 0))],
        out_specs=pl.BlockSpec((1, 1, capp, d), lambda e, bi, u: (bi, e, 0, 0)),
    )
    return pl.pallas_call(
        functools.partial(_ffn_kernel, rb=rb),
        grid_spec=grid_spec,
        out_shape=jax.ShapeDtypeStruct((b, e_n, capp, d), BF16),
        compiler_params=_cparams("arbitrary", "arbitrary"),
    )(used, xe, wg, wu, wd)


def _combine_kernel(nrows_ref, table_ref, next_table_ref, goff_ref, next_goff_ref, ridx_ref,
                    next_ridx_ref, x_ref, g2_ref, fg_ref, ye_ref, o_ref,
                    stage, extra, oh, oh_extra, acc, sem, xsem, *, tr, final):
    bi, ti = pl.program_id(0), pl.program_id(1)
    step = bi * pl.num_programs(1) + ti
    n_steps = pl.num_programs(0) * pl.num_programs(1)
    par = step % 2

    def chunks(s):
        return (nrows_ref[s] + (ROW_CHUNK - 1)) // ROW_CHUNK

    nch = chunks(step)
    n_pre = jnp.minimum(nch, COMBINE_AHEAD)

    def piece(tbl, c, p, dst, dsem):
        src = tbl[0, 0, c * PIECES + p]
        return pltpu.make_async_copy(
            ye_ref.at[pl.ds(pl.multiple_of(jnp.maximum(src, 0), ROW_ALIGN), ROW_ALIGN), :],
            dst.at[pl.ds(p * ROW_ALIGN, ROW_ALIGN), :], dsem)

    def fetch_tile(tbl, parity, n):
        for c in range(COMBINE_AHEAD):
            @pl.when(c < n)
            def _(c=c):
                for p in range(PIECES):
                    piece(tbl, c, p, stage.at[parity, c], sem.at[parity, c]).start()

    def build_tile(rref, gref, parity):
        return [_build_one_hot(oh.at[parity], c, rref, gref, c, tr) for c in range(COMBINE_AHEAD)]

    def patch_tile(rref, parity, spans):
        for c, span in enumerate(spans):
            _patch_one_hot(oh.at[parity], c, rref, c, *span, tr)

    @pl.when(step == 0)
    def _():
        stage[...] = jnp.zeros(stage.shape, BF16)
        fetch_tile(table_ref, 0, n_pre)
        patch_tile(ridx_ref, 0, build_tile(ridx_ref, goff_ref, 0))

    @pl.when(step + 1 < n_steps)
    def _():
        fetch_tile(next_table_ref, 1 - par, jnp.minimum(chunks(step + 1), COMBINE_AHEAD))

    for c in range(COMBINE_AHEAD):
        @pl.when(c < n_pre)
        def _(c=c):
            for p in range(PIECES):
                piece(table_ref, c, p, stage.at[par, c], sem.at[par, c]).wait()

    next_spans = build_tile(next_ridx_ref, next_goff_ref, 1 - par)
    rows = COMBINE_AHEAD * ROW_CHUNK
    acc[...] = _dot_tn(oh[par].reshape(rows, oh.shape[3]), stage[par].reshape(rows, stage.shape[3]))
    patch_tile(next_ridx_ref, 1 - par, next_spans)

    def one_more(c, carry):
        for p in range(PIECES):
            piece(table_ref, c, p, extra, xsem).start()
        span = _build_one_hot(oh_extra, 0, ridx_ref, goff_ref, c, tr)
        _patch_one_hot(oh_extra, 0, ridx_ref, c, *span, tr)
        for p in range(PIECES):
            piece(table_ref, c, p, extra, xsem).wait()
        acc[...] += _dot_tn(oh_extra[0], extra[...])
        return carry

    lax.fori_loop(COMBINE_AHEAD, nch, one_more, 0)
    xo = x_ref[0] + g2_ref[0] * acc[...]
    if final:
        xo = xo * lax.rsqrt(jnp.mean(xo * xo, axis=-1, keepdims=True) + EPS) * fg_ref[...]
    o_ref[0] = xo


def _combine(ye_flat, ridx5, table3, goff3, nrows, x, g2, fg, tt, final):
    b, t, d = x.shape
    nt = t // tt
    tr = tt // LANES
    pl_lanes = table3.shape[2]
    nxt = lambda bi, ti: jnp.minimum(bi * nt + ti + 1, b * nt - 1)
    smem = lambda width, index: pl.BlockSpec((1, 1, width), index, memory_space=pltpu.SMEM)
    grid_spec = pltpu.PrefetchScalarGridSpec(
        num_scalar_prefetch=1,
        grid=(b, nt),
        in_specs=[smem(pl_lanes, lambda bi, ti, nr: (bi * nt + ti, 0, 0)),
                  smem(pl_lanes, lambda bi, ti, nr: (nxt(bi, ti), 0, 0)),
                  smem(LANES, lambda bi, ti, nr: (bi * nt + ti, 0, 0)),
                  smem(LANES, lambda bi, ti, nr: (nxt(bi, ti), 0, 0)),
                  pl.BlockSpec((1, N_EXPERTS, 1, tr, LANES), lambda bi, ti, nr: (bi, 0, ti, 0, 0)),
                  pl.BlockSpec((1, N_EXPERTS, 1, tr, LANES),
                               lambda bi, ti, nr: (nxt(bi, ti) // nt, 0, nxt(bi, ti) % nt, 0, 0)),
                  pl.BlockSpec((1, tt, d), lambda bi, ti, nr: (bi, ti, 0)),
                  pl.BlockSpec((1, 1, d), lambda bi, ti, nr: (bi, 0, 0)),
                  pl.BlockSpec((1, d), lambda bi, ti, nr: (0, 0)),
                  pl.BlockSpec(memory_space=pl.ANY)],
        out_specs=pl.BlockSpec((1, tt, d), lambda bi, ti, nr: (bi, ti, 0)),
        scratch_shapes=[pltpu.VMEM((2, COMBINE_AHEAD, ROW_CHUNK, d), BF16),
                        pltpu.VMEM((ROW_CHUNK, d), BF16),
                        pltpu.VMEM((2, COMBINE_AHEAD, ROW_CHUNK, tt), BF16),
                        pltpu.VMEM((1, ROW_CHUNK, tt), BF16), pltpu.VMEM((tt, d), F32),
                        pltpu.SemaphoreType.DMA((2, COMBINE_AHEAD)), pltpu.SemaphoreType.DMA(())],
    )
    return pl.pallas_call(
        functools.partial(_combine_kernel, tr=tr, final=final),
        grid_spec=grid_spec,
        out_shape=jax.ShapeDtypeStruct((b, t, d), F32),
        compiler_params=_cparams("arbitrary", "arbitrary"),
    )(nrows.reshape(b * nt), table3, table3, goff3, goff3, ridx5, ridx5, x, g2, fg, ye_flat)


def _moe_tiles(t):
    tt = min(512, t)
    t_sel = max(t, 8 * LANES)
    return tt, t_sel


def _ec_moe(x, g, shift, scale, gate, wr_pad, wg, wu, wd, layer, final_g, final):
    b, t, d = x.shape
    e_n = N_EXPERTS
    cap = EC_CAPACITY_FACTOR * t // e_n
    tt, t_sel = _moe_tiles(t)
    nt, tr = t // tt, tt // LANES
    capp = cap + ROW_ALIGN * nt
    rb = min(ROW_CHUNK, capp)
    capp = -(-capp // rb) * rb
    pl_lanes = -(-(tt + ROW_ALIGN) // LANES) * LANES

    h_ext, aff_t = _router(x, g, shift, scale, wr_pad, min(512, t))
    if t_sel > t:
        aff_t = jnp.pad(aff_t, ((0, 0), (0, 0), (0, t_sel - t)), constant_values=-1.0)
    aff_r = aff_t.reshape(b, e_n, t_sel // LANES, LANES)
    ridx, table, goff, used = _select(aff_r, cap, tr, capp, pl_lanes)
    ridx5 = ridx[:, :, :nt * tr].reshape(b, e_n, nt, tr, LANES)
    table3 = table[:, :nt * tr:tr].reshape(b * nt, 1, pl_lanes)
    goff3 = goff[:, :nt * tr:tr].reshape(b * nt, 1, LANES)
    nrows2 = goff[:, :nt * tr:tr, e_n]
    used2 = used[:, :, 0, 0]

    xe = _dispatch(h_ext, ridx5, table3, goff3, nrows2, used2, capp, tt)
    ye = _ffn(xe, used2, wg, wu, wd, layer, capp, rb)
    return _combine(ye.reshape(b * e_n * capp, d), ridx5, table3, goff3, nrows2, x, gate, final_g, tt,
                    final)


def kernel(x, c, ctx, c_ctx, ada_w, ada_b, norm_g, gla_w_in, gla_w_a2, gla_b_a2, gla_norm_g,
           gla_w_out, conv_w_in, conv_k, conv_w_out, router_w, expert_w_gate, expert_w_up,
           expert_w_down, final_norm_g):
    b, t, d = x.shape
    depth = ada_w.shape[0]
    assert depth == 2 and b + 1 <= 8
    dk = gla_w_a2.shape[3]
    dv = gla_w_out.shape[1]
    hk, hv = dk // GLA_HEADS, dv // GLA_HEADS

    cvec = jnp.zeros((8, d), F32).at[:b].set(c).at[b].set(c_ctx)
    mods = _ada(cvec, ada_w, ada_b)

    def mod_vectors(layer, rows):
        m = mods[layer, rows].reshape(rows.shape[0], 1, 6, d)
        return [m[:, :, i, :] for i in range(6)]

    rows_x = jnp.arange(b)
    rows_c = jnp.full((b,), b)
    wr_hi = router_w.astype(BF16)
    wr_lo = (router_w - wr_hi.astype(F32)).astype(BF16)
    wr_pad = [jnp.pad(jnp.concatenate([wr_hi[i], wr_lo[i]], axis=1), ((0, 0), (0, LANES - 2 * N_EXPERTS)))
              for i in range(depth)]
    fg = final_norm_g.reshape(1, d)

    sh1x, sc1x, g1x, sh2x, sc2x, g2x = mod_vectors(0, rows_x)
    sh1c, sc1c, g1c, sh2c, sc2c, g2c = mod_vectors(0, rows_c)
    n_main = 2 * dk + 2 * dv
    w_main = jnp.pad(gla_w_in[0], ((0, 0), (0, LANES - 2 * GLA_RANK))).astype(BF16)
    assert w_main.shape[1] == n_main + LANES
    w2 = jnp.zeros((2 * GLA_RANK, 2 * dk), F32)
    w2 = w2.at[:GLA_RANK, :dk].set(gla_w_a2[0, 0]).at[GLA_RANK:, dk:].set(gla_w_a2[0, 1])
    w2_hi = w2.astype(BF16)
    w2_lo = (w2 - w2_hi.astype(F32)).astype(BF16)
    w2 = jnp.concatenate([w2_hi, w2_hi, w2_lo, jnp.zeros((LANES - 6 * GLA_RANK, 2 * dk), BF16)])
    b2 = gla_b_a2[0].reshape(1, 2 * dk)
    gn = gla_norm_g[0].reshape(1, hv)
    w_out = gla_w_out[0].astype(BF16)
    g_n1 = norm_g[0, 0].reshape(1, d)
    g_n2 = norm_g[0, 1].reshape(1, d)

    tc = ctx.shape[1]
    qc, kc, vc, rc, lac = _gla_in(ctx, g_n1, sh1c, sc1c, w_main, w2, b2, min(512, tc))
    s0 = jnp.zeros((b, GLA_HEADS, hv, hk), F32)
    ocf, ocb, s_cf, s_cb = _gla_scan(qc, kc, vc, lac, s0, s0)
    qx, kx, vx, rx, lax_ = _gla_in(x, g_n1, sh1x, sc1x, w_main, w2, b2, 512)
    experts = (expert_w_gate, expert_w_up, expert_w_down)
    groups = b * GLA_HEADS * (t // GLA_CHUNK) // CAST_GROUP
    mats = depth * N_EXPERTS
    if (groups * CAST_GROUP == b * GLA_HEADS * (t // GLA_CHUNK) and groups % mats == 0
            and all(w.shape[2] % (groups // mats * ROW_ALIGN) == 0 for w in experts)):
        oxf, oxb, _, _, wg, wu, wd = _gla_scan(qx, kx, vx, lax_, s_cf, s_cb, cast=experts)
        wg, wu, wd = (w.reshape(s.shape) for w, s in zip((wg, wu, wd), experts))
    else:
        oxf, oxb, _, _ = _gla_scan(qx, kx, vx, lax_, s_cf, s_cb)
        wg, wu, wd = (w.astype(BF16) for w in experts)
    x = _gla_out(oxf, oxb, rx, x, gn, w_out, g1x, 512)
    x = _ec_moe(x, g_n2, sh2x, sc2x, g2x, wr_pad[0], wg, wu, wd, 0, fg, False)

    ctx = _gla_out(ocf, ocb, rc, ctx, gn, w_out, g1c, min(512, tc))
    ctx = _ec_moe(ctx, g_n2, sh2c, sc2c, g2c, wr_pad[0], wg, wu, wd, 0, fg, False)
    del ctx

    sh1x, sc1x, g1x, sh2x, sc2x, g2x = mod_vectors(1, rows_x)
    x = _conv_mix(x, norm_g[1, 0].reshape(1, d), sh1x, sc1x, conv_w_in[0].astype(BF16), conv_k[0],
                  conv_w_out[0].astype(BF16), g1x, GRID_W, 512)
    return _ec_moe(x, norm_g[1, 1].reshape(1, d), sh2x, sc2x, g2x, wr_pad[1], wg, wu, wd, 1, fg, True)
```

```python
import functools

import jax
import jax.numpy as jnp
from jax import lax
from jax.experimental import pallas as pl
from jax.experimental.pallas import tpu as pltpu

F32 = jnp.float32
BF16 = jnp.bfloat16
I32 = jnp.int32
HIGHEST = lax.Precision.HIGHEST

EPS = 1e-6
GRID_W = 64
GLA_HEADS = 4
GLA_RANK = 16
GLA_TAU = 16.0
N_EXPERTS = 16
EC_CAPACITY_FACTOR = 2
CONV_WIDTH = 3

LANES = 128
ROW_ALIGN = 16
GLA_CHUNK = 256
GLA_DIAG = 8
ROW_CHUNK = 256
PIECES = ROW_CHUNK // ROW_ALIGN
ONE_HOT_SLOTS = 3
DISPATCH_CHUNKS = 5
TRASH_ROWS = (2 * DISPATCH_CHUNKS + 1) * ROW_CHUNK
COMBINE_AHEAD = 5
MIN_NORMAL_BITS = 0x00800000
REFINE_STEPS = 40
VMEM_LIMIT = 60 * 1024 * 1024


def _cparams(*sem):
    return pltpu.CompilerParams(dimension_semantics=sem, vmem_limit_bytes=VMEM_LIMIT)


def _dot(a, b):
    return jnp.dot(a, b, preferred_element_type=F32)


def _dot_nt(a, b):
    return lax.dot_general(a, b, (((1,), (1,)), ((), ())), preferred_element_type=F32)


def _dot_tn(a, b):
    return lax.dot_general(a, b, (((0,), (0,)), ((), ())), preferred_element_type=F32)


def _xdot(a, b):
    return jnp.dot(a, b, precision=HIGHEST, preferred_element_type=F32)


def _split2(x):
    hi = x.astype(BF16)
    return hi, (x - hi.astype(F32)).astype(BF16)


def _silu(x):
    return x * jax.nn.sigmoid(x)


def _norm_mod(x, g, shift, scale):
    y = x * lax.rsqrt(jnp.mean(x * x, axis=-1, keepdims=True) + EPS) * g
    return y * (1.0 + scale) + shift


def _split3(x):
    hi = x.astype(BF16)
    r1 = x - hi.astype(F32)
    mid = r1.astype(BF16)
    lo = (r1 - mid.astype(F32)).astype(BF16)
    return hi, mid, lo


def _ada_kernel(c_ref, w_ref, b_ref, o_ref):
    o_ref[0] = _xdot(_silu(c_ref[...]), w_ref[0]) + b_ref[0]


def _ada(cvec, ada_w, ada_b):
    depth, d, n6 = ada_w.shape
    nt = n6 // d
    return pl.pallas_call(
        _ada_kernel,
        grid=(depth, nt),
        in_specs=[pl.BlockSpec((8, d), lambda l, n: (0, 0)),
                  pl.BlockSpec((1, d, d), lambda l, n: (l, 0, n)),
                  pl.BlockSpec((1, 1, d), lambda l, n: (l, 0, n))],
        out_specs=pl.BlockSpec((1, 8, d), lambda l, n: (l, 0, n)),
        out_shape=jax.ShapeDtypeStruct((depth, 8, n6), F32),
        compiler_params=_cparams("arbitrary", "arbitrary"),
    )(cvec, ada_w, ada_b.reshape(depth, 1, n6))


def _gla_in_kernel(x_ref, g_ref, sh_ref, sc_ref, w_ref, w2_ref, b2_ref,
                   q_ref, k_ref, v_ref, r_ref, la_ref, *, dk, dv):
    h = _norm_mod(x_ref[0], g_ref[...], sh_ref[0], sc_ref[0])
    y = _dot(h.astype(BF16), w_ref[...])
    hk = dk // GLA_HEADS
    q_ref[0] = y[:, :dk] * (hk ** -0.5)
    k_ref[0] = y[:, dk:2 * dk]
    v_ref[0] = y[:, 2 * dk:2 * dk + dv]
    r_ref[0] = y[:, 2 * dk + dv:2 * dk + 2 * dv].astype(BF16)
    a_hi, a_lo = _split2(y[:, 2 * dk + 2 * dv:])
    a3 = (a_hi.astype(F32) + pltpu.roll(a_lo.astype(F32), 2 * GLA_RANK, 1)
          + pltpu.roll(a_hi.astype(F32), 4 * GLA_RANK, 1))
    z = _dot(a3.astype(BF16), w2_ref[...]) + b2_ref[...]
    log_sig = jnp.minimum(z, 0.0) - jnp.log(1.0 + jnp.exp(-jnp.abs(z)))
    la_ref[0] = log_sig * (1.0 / GLA_TAU)


def _gla_in(x, g, shift, scale, w_main, w2, b2, tm):
    b, t, d = x.shape
    dk = w2.shape[1] // 2
    dv = (w_main.shape[1] - LANES - 2 * dk) // 2
    nw = w_main.shape[1]
    tok = lambda width: pl.BlockSpec((1, tm, width), lambda bi, ti: (bi, ti, 0))
    vec = pl.BlockSpec((1, 1, d), lambda bi, ti: (bi, 0, 0))
    full = lambda shape: pl.BlockSpec(shape, lambda bi, ti: tuple(0 for _ in shape))
    return pl.pallas_call(
        functools.partial(_gla_in_kernel, dk=dk, dv=dv),
        grid=(b, t // tm),
        in_specs=[tok(d), full((1, d)), vec, vec, full((d, nw)), full((LANES, 2 * dk)), full((1, 2 * dk))],
        out_specs=[tok(dk), tok(dk), tok(dv), tok(dv), tok(2 * dk)],
        out_shape=[jax.ShapeDtypeStruct((b, t, dk), F32), jax.ShapeDtypeStruct((b, t, dk), F32),
                   jax.ShapeDtypeStruct((b, t, dv), F32), jax.ShapeDtypeStruct((b, t, dv), BF16),
                   jax.ShapeDtypeStruct((b, t, 2 * dk), F32)],
        compiler_params=_cparams("arbitrary", "arbitrary"),
    )(x, g, shift, scale, w_main, w2, b2)


def _gla_chunk(q, k, v, la, st, rev):
    c, hk = q.shape
    ii = lax.broadcasted_iota(I32, (c, c), 0)
    jj = lax.broadcasted_iota(I32, (c, c), 1)
    tri = ((jj >= ii) if rev else (jj <= ii)).astype(BF16)
    hi, lo = _split2(la)
    acum = _dot(tri, hi) + _dot(tri, lo)
    a_last = acum[0:1] if rev else acum[c - 1:c]

    o = _dot_nt((q * jnp.exp(acum)).astype(BF16), st.astype(BF16))

    xor = ii ^ jj
    rowi = lax.broadcasted_iota(I32, (c, hk), 0)
    p = jnp.zeros((c, c), F32)
    blk = c
    while blk >= 2 * GLA_DIAG:
        half = blk // 2
        a3 = acum.reshape(c // blk, blk, hk)
        ridx = half if rev else half - 1
        ref = jnp.broadcast_to(a3[:, ridx:ridx + 1, :], a3.shape).reshape(c, hk)
        pos = rowi & (blk - 1)
        late = (pos < half) if rev else (pos >= half)
        qe = jnp.where(late, q * jnp.exp(acum - ref), 0.0).astype(BF16)
        ke = jnp.where(late, 0.0, k * jnp.exp(ref - acum)).astype(BF16)
        raw = _dot_nt(qe, ke)
        p = p + (raw if blk == c else jnp.where(xor < blk, raw, 0.0))
        blk = half

    nb = c // GLA_DIAG
    q8 = q.reshape(nb, GLA_DIAG, hk)
    k8 = k.reshape(nb, GLA_DIAG, hk)
    a8 = acum.reshape(nb, GLA_DIAG, hk)
    sub = lax.broadcasted_iota(I32, (nb, GLA_DIAG, hk), 1)
    colmod = jj & (GLA_DIAG - 1)
    pd = jnp.zeros((c, c), F32)
    for j in range(GLA_DIAG):
        kj = jnp.broadcast_to(k8[:, j:j + 1, :], k8.shape)
        aj = jnp.broadcast_to(a8[:, j:j + 1, :], a8.shape)
        valid = (sub <= j) if rev else (sub >= j)
        dec = jnp.where(valid, jnp.exp(a8 - aj), 0.0)
        cj = jnp.sum(q8 * kj * dec, axis=2, keepdims=True).reshape(c, 1)
        pd = jnp.where(colmod == j, cj, pd)
    p = p + jnp.where(xor < GLA_DIAG, pd, 0.0)

    vb = v.astype(BF16)
    o = o + _dot(p.astype(BF16), vb)
    kd = (k * jnp.exp(a_last - acum)).astype(BF16)
    st_new = st * jnp.exp(a_last) + _dot_tn(vb, kd)
    return o, st_new


def _gla_scan_kernel(*refs, n_cast):
    (qf_ref, kf_ref, vf_ref, laf_ref, qb_ref, kb_ref, vb_ref, lab_ref, s0f_ref, s0b_ref) = refs[:10]
    w_in = refs[10:10 + n_cast]
    of_ref, ob_ref, sf_ref, sb_ref = refs[10 + n_cast:14 + n_cast]
    w_out = refs[14 + n_cast:14 + 2 * n_cast]
    stf, stb = refs[14 + 2 * n_cast:]
    n = pl.program_id(2)
    for wi, wo in zip(w_in, w_out):
        wo[0] = wi[0, 0].astype(BF16)

    @pl.when(n == 0)
    def _():
        stf[...] = s0f_ref[0, 0]
        stb[...] = s0b_ref[0, 0]

    o, s = _gla_chunk(qf_ref[0], kf_ref[0], vf_ref[0], laf_ref[0], stf[...], False)
    of_ref[0] = o.astype(BF16)
    stf[...] = s
    o, s = _gla_chunk(qb_ref[0], kb_ref[0], vb_ref[0], lab_ref[0], stb[...], True)
    ob_ref[0] = o.astype(BF16)
    stb[...] = s

    @pl.when(n == pl.num_programs(2) - 1)
    def _():
        sf_ref[0, 0] = stf[...]
        sb_ref[0, 0] = stb[...]


def _gla_scan(q, k, v, la, s0f, s0b, cast=()):
    b, t, dk = q.shape
    dv = v.shape[2]
    hk, hv = dk // GLA_HEADS, dv // GLA_HEADS
    c = GLA_CHUNK
    nc = t // c
    steps = b * GLA_HEADS * nc
    fwd = lambda w, off: pl.BlockSpec((1, c, w), lambda bi, h, n: (bi, n, h + off))
    bwd = lambda w, off: pl.BlockSpec((1, c, w), lambda bi, h, n: (bi, nc - 1 - n, h + off))
    st = pl.BlockSpec((1, 1, hv, hk), lambda bi, h, n: (bi, h, 0, 0))
    cast_in, cast_out, cast_shape = [], [], []
    for w in cast:
        n_l, n_e, rows, cols = w.shape
        per = steps // (n_l * n_e)
        rb = rows // per
        assert per * n_l * n_e == steps and rb * per == rows and rb % ROW_ALIGN == 0
        step = lambda bi, h, n: (bi * GLA_HEADS + h) * nc + n
        cast_in.append(pl.BlockSpec(
            (1, 1, rb, cols),
            lambda bi, h, n, per=per, n_e=n_e: (step(bi, h, n) // per // n_e, step(bi, h, n) // per % n_e,
                                                step(bi, h, n) % per, 0)))
        cast_out.append(pl.BlockSpec(
            (1, rb, cols), lambda bi, h, n, per=per: (step(bi, h, n) // per, step(bi, h, n) % per, 0)))
        cast_shape.append(jax.ShapeDtypeStruct((n_l * n_e, rows, cols), BF16))
    return pl.pallas_call(
        functools.partial(_gla_scan_kernel, n_cast=len(cast)),
        grid=(b, GLA_HEADS, nc),
        in_specs=[fwd(hk, 0), fwd(hk, 0), fwd(hv, 0), fwd(hk, 0),
                  bwd(hk, 0), bwd(hk, 0), bwd(hv, 0), bwd(hk, GLA_HEADS), st, st] + cast_in,
        out_specs=[fwd(hv, 0), bwd(hv, 0), st, st] + cast_out,
        out_shape=[jax.ShapeDtypeStruct((b, t, dv), BF16), jax.ShapeDtypeStruct((b, t, dv), BF16),
                   jax.ShapeDtypeStruct((b, GLA_HEADS, hv, hk), F32),
                   jax.ShapeDtypeStruct((b, GLA_HEADS, hv, hk), F32)] + cast_shape,
        scratch_shapes=[pltpu.VMEM((hv, hk), F32), pltpu.VMEM((hv, hk), F32)],
        compiler_params=_cparams("arbitrary", "arbitrary", "arbitrary"),
    )(q, k, v, la, q, k, v, la, s0f, s0b, *cast)


def _gla_out_kernel(of_ref, ob_ref, r_ref, x_ref, gn_ref, w_ref, g1_ref, o_ref):
    o = of_ref[0].astype(F32) + ob_ref[0].astype(F32)
    hv = gn_ref.shape[1]
    parts = []
    for h in range(GLA_HEADS):
        oh = o[:, h * hv:(h + 1) * hv]
        ms = jnp.mean(oh * oh, axis=-1, keepdims=True)
        parts.append(oh * lax.rsqrt(ms + EPS) * gn_ref[...])
    y = jnp.concatenate(parts, axis=1) * _silu(r_ref[0].astype(F32))
    o_ref[0] = x_ref[0] + g1_ref[0] * _dot(y.astype(BF16), w_ref[...])


def _gla_out(o_f, o_b, r, x, gn, w_out, g1, tm):
    b, t, d = x.shape
    dv = o_f.shape[2]
    tok = lambda width: pl.BlockSpec((1, tm, width), lambda bi, ti: (bi, ti, 0))
    return pl.pallas_call(
        _gla_out_kernel,
        grid=(b, t // tm),
        in_specs=[tok(dv), tok(dv), tok(dv), tok(d),
                  pl.BlockSpec((1, dv // GLA_HEADS), lambda bi, ti: (0, 0)),
                  pl.BlockSpec((dv, d), lambda bi, ti: (0, 0)),
                  pl.BlockSpec((1, 1, d), lambda bi, ti: (bi, 0, 0))],
        out_specs=tok(d),
        out_shape=jax.ShapeDtypeStruct((b, t, d), F32),
        compiler_params=_cparams("arbitrary", "arbitrary"),
    )(o_f, o_b, r, x, gn, w_out, g1)


def _conv_mix_kernel(x_ref, g_ref, sh_ref, sc_ref, win_ref, ck_ref, wout_ref, g1_ref, o_ref, *, seg):
    x = x_ref[0]
    tm, d = x.shape
    h = _norm_mod(x, g_ref[...], sh_ref[0], sc_ref[0])
    y = _dot(h.astype(BF16), win_ref[...])
    bg, cg, v = y[:, :d], y[:, d:2 * d], y[:, 2 * d:]
    u = cg * v
    pos = lax.broadcasted_iota(I32, (tm, 1), 0) % seg
    u_prev = jnp.where(pos == 0, 0.0, pltpu.roll(u, 1, 0))
    u_next = jnp.where(pos == seg - 1, 0.0, pltpu.roll(u, tm - 1, 0))
    conv = u_prev * ck_ref[0:1, :] + u * ck_ref[1:2, :] + u_next * ck_ref[2:3, :]
    o_ref[0] = x + g1_ref[0] * _dot((bg * conv).astype(BF16), wout_ref[...])


def _conv_mix(x, g, shift, scale, w_in, ck, w_out, g1, seg, tm):
    b, t, d = x.shape
    tok = pl.BlockSpec((1, tm, d), lambda bi, ti: (bi, ti, 0))
    vec = pl.BlockSpec((1, 1, d), lambda bi, ti: (bi, 0, 0))
    full = lambda shape: pl.BlockSpec(shape, lambda bi, ti: tuple(0 for _ in shape))
    return pl.pallas_call(
        functools.partial(_conv_mix_kernel, seg=seg),
        grid=(b, t // tm),
        in_specs=[tok, full((1, d)), vec, vec, full((d, 3 * d)), full((CONV_WIDTH, d)), full((d, d)), vec],
        out_specs=tok,
        out_shape=jax.ShapeDtypeStruct((b, t, d), F32),
        compiler_params=_cparams("arbitrary", "arbitrary"),
    )(x, g, shift, scale, w_in, ck, w_out, g1)


def _router_kernel(x_ref, g_ref, sh_ref, sc_ref, wr_ref, h_ref, aff_ref):
    d = x_ref.shape[2]
    h = _norm_mod(x_ref[0], g_ref[...], sh_ref[0], sc_ref[0])
    h_hi, h_lo = _split2(h)
    p = _dot(h_hi, wr_ref[...]) + _dot(h_lo, wr_ref[...])
    logits = p + pltpu.roll(p, LANES - N_EXPERTS, 1)
    lane = lax.broadcasted_iota(I32, logits.shape, 1)
    is_e = lane < N_EXPERTS
    m = jnp.max(jnp.where(is_e, logits, -jnp.inf), axis=-1, keepdims=True)
    ex = jnp.where(is_e, jnp.exp(logits - m), 0.0)
    aff = ex / jnp.sum(ex, axis=-1, keepdims=True)
    aff_ref[0] = aff.T[:N_EXPERTS, :]
    hi, mid, lo = _split3(aff)
    ext = (hi.astype(F32) + pltpu.roll(mid.astype(F32), N_EXPERTS, 1)
           + pltpu.roll(lo.astype(F32), 2 * N_EXPERTS, 1))
    h_ref[0, :, :d] = h.astype(BF16)
    h_ref[0, :, d:] = ext.astype(BF16)


def _router(x, g, shift, scale, wr_pad, tm):
    b, t, d = x.shape
    tok = lambda width: pl.BlockSpec((1, tm, width), lambda bi, ti: (bi, ti, 0))
    vec = pl.BlockSpec((1, 1, d), lambda bi, ti: (bi, 0, 0))
    return pl.pallas_call(
        _router_kernel,
        grid=(b, t // tm),
        in_specs=[tok(d), pl.BlockSpec((1, d), lambda bi, ti: (0, 0)), vec, vec,
                  pl.BlockSpec((d, LANES), lambda bi, ti: (0, 0))],
        out_specs=[tok(d + LANES), pl.BlockSpec((1, N_EXPERTS, tm), lambda bi, ti: (bi, 0, ti))],
        out_shape=[jax.ShapeDtypeStruct((b, t, d + LANES), BF16),
                   jax.ShapeDtypeStruct((b, N_EXPERTS, t), F32)],
        compiler_params=_cparams("arbitrary", "arbitrary"),
    )(x, g, shift, scale, wr_pad)


def _select_kernel(aff_ref, ridx_ref, table_ref, goff_ref, used_ref, *, cap, tr, capp, pl_lanes):
    e_n = N_EXPERTS
    a = aff_ref[0]
    r_n = a.shape[1]

    def count_ge(v):
        return jnp.sum(jnp.where(a >= v, 1.0, 0.0), axis=(1, 2), keepdims=True)

    def search(it, cur):
        cand = cur | (jnp.int32(1) << (30 - it))
        return jnp.where(count_ge(lax.bitcast_convert_type(cand, F32)) >= cap, cand, cur)

    thr = lax.fori_loop(0, 31, search, jnp.zeros((e_n, 1, 1), I32))
    normal = thr >= MIN_NORMAL_BITS
    lo = lax.bitcast_convert_type(jnp.where(normal, thr, 0), F32)
    hi = lax.bitcast_convert_type(jnp.where(normal, thr + 1, MIN_NORMAL_BITS), F32)

    def refine(it, lh):
        lo_, hi_ = lh
        mid = 0.5 * (lo_ + hi_)
        ok = count_ge(mid) >= cap
        return jnp.where(ok, mid, lo_), jnp.where(ok, hi_, mid)

    lo, hi = lax.fori_loop(0, REFINE_STEPS, refine, (lo, hi))
    gt = a >= hi
    eq = (a >= lo) & jnp.logical_not(gt)
    need = cap - jnp.sum(jnp.where(gt, 1.0, 0.0), axis=(1, 2), keepdims=True)

    li = lax.broadcasted_iota(I32, (LANES, LANES), 0)
    lj = lax.broadcasted_iota(I32, (LANES, LANES), 1)
    upper = (li <= lj).astype(BF16)
    ones = jnp.ones((LANES, LANES), BF16)
    ri = lax.broadcasted_iota(I32, (r_n, r_n), 0)
    rj = lax.broadcasted_iota(I32, (r_n, r_n), 1)
    strict_lower = (rj < ri).astype(BF16)

    def cumsum_tokens(mask):
        m = jnp.where(mask, 1.0, 0.0).astype(BF16).reshape(e_n * r_n, LANES)
        within = _dot(m, upper).reshape(e_n, r_n, LANES)
        rs = _dot(m, ones).reshape(e_n, r_n, LANES)
        pre = jnp.stack([_dot(strict_lower, rs[e].astype(BF16)) for e in range(e_n)])
        return within + pre, rs

    eq_rank, _ = cumsum_tokens(eq)
    sel = gt | (eq & (eq_rank <= need))
    incl, rs = cumsum_tokens(sel)
    pos = incl - jnp.where(sel, 1.0, 0.0)

    same_tile = ((ri // tr) == (rj // tr)).astype(F32)
    prev_tile = ((rj // tr) < (ri // tr)).astype(F32)
    first_row = (rj % tr) == 0
    prev_first = jnp.where(first_row, prev_tile, 0.0)
    all_first = jnp.where(first_row, 1.0, 0.0)

    goff = jnp.zeros((r_n, LANES), F32)
    lane = lax.broadcasted_iota(I32, (r_n, LANES), 1)
    gtab = jnp.zeros((r_n, LANES), F32)
    rows = []
    tbl = jnp.full((r_n, pl_lanes), -1.0, F32)
    row16 = (lax.broadcasted_iota(I32, (r_n, pl_lanes), 1) * ROW_ALIGN).astype(F32)
    rep = lambda x: jnp.concatenate([x] * (pl_lanes // LANES), axis=1)
    base = (pl.program_id(0) * e_n * capp).astype(F32)
    for e in range(e_n):
        n_tile = _xdot(same_tile, rs[e])
        pos_start = _xdot(prev_tile, rs[e])
        n_pad = jnp.floor((n_tile + (ROW_ALIGN - 1)) * (1.0 / ROW_ALIGN)) * ROW_ALIGN
        off = _xdot(prev_first, n_pad)
        used_ref[0, e] = _xdot(all_first, n_pad).astype(I32)
        rows.append(jnp.where(sel[e], goff + pos[e] - pos_start, -1.0))
        gtab = jnp.where(lane == e, goff, gtab)
        g0, g1 = rep(goff), rep(goff + n_pad)
        inside = (row16 >= g0) & (row16 < g1)
        tbl = jnp.where(inside, base + e * capp + rep(off) + row16 - g0, tbl)
        goff = goff + n_pad
    ridx_ref[0] = jnp.stack(rows).astype(I32)
    table_ref[0] = tbl.astype(I32)
    goff_ref[0] = jnp.where(lane >= e_n, goff, gtab).astype(I32)


def _select(aff_r, cap, tr, capp, pl_lanes):
    b, e_n, r_n, _ = aff_r.shape
    blk4 = pl.BlockSpec((1, e_n, r_n, LANES), lambda bi: (bi, 0, 0, 0))
    return pl.pallas_call(
        functools.partial(_select_kernel, cap=cap, tr=tr, capp=capp, pl_lanes=pl_lanes),
        grid=(b,),
        in_specs=[blk4],
        out_specs=[blk4, pl.BlockSpec((1, r_n, pl_lanes), lambda bi: (bi, 0, 0)),
                   pl.BlockSpec((1, r_n, LANES), lambda bi: (bi, 0, 0)), blk4],
        out_shape=[jax.ShapeDtypeStruct((b, e_n, r_n, LANES), I32),
                   jax.ShapeDtypeStruct((b, r_n, pl_lanes), I32),
                   jax.ShapeDtypeStruct((b, r_n, LANES), I32),
                   jax.ShapeDtypeStruct((b, e_n, r_n, LANES), I32)],
        compiler_params=_cparams("arbitrary"),
    )(aff_r)


def _expert_span(goff_ref, base):
    e_lo = jnp.int32(0)
    e_hi = jnp.int32(0)
    for e in range(N_EXPERTS):
        e_lo += (goff_ref[0, 0, e + 1] <= base).astype(I32)
        e_hi += (goff_ref[0, 0, e] < base + ROW_CHUNK).astype(I32)
    return e_lo, e_hi


def _one_hot_rows(ridx_ref, base, e_lo, e_hi, tr):
    riota = lax.broadcasted_iota(I32, (ROW_CHUNK, LANES), 0) + base
    pieces = []
    for q in range(tr):
        acc = jnp.zeros((ROW_CHUNK, LANES), F32)
        for k in range(ONE_HOT_SLOTS):
            e = jnp.minimum(e_lo + k, N_EXPERTS - 1)
            row = jnp.where(e_lo + k < e_hi, ridx_ref[0, e, 0, pl.ds(q, 1), :], -2)
            acc = jnp.where(row == riota, 1.0, acc)
        pieces.append(acc.astype(BF16))
    return jnp.concatenate(pieces, axis=1)


def _one_hot_rows_any(ridx_ref, base, e_lo, e_hi, tr):
    riota = lax.broadcasted_iota(I32, (ROW_CHUNK, LANES), 0) + base
    pieces = []
    for q in range(tr):
        def add_expert(e, acc, q=q):
            return jnp.where(ridx_ref[0, e, 0, pl.ds(q, 1), :] == riota, 1.0, acc)
        acc = lax.fori_loop(e_lo, e_hi, add_expert, jnp.zeros((ROW_CHUNK, LANES), F32))
        pieces.append(acc.astype(BF16))
    return jnp.concatenate(pieces, axis=1)


def _build_one_hot(oh_ref, slot, ridx_ref, goff_ref, c, tr):
    base = c * ROW_CHUNK
    e_lo, e_hi = _expert_span(goff_ref, base)
    oh_ref[slot] = _one_hot_rows(ridx_ref, base, e_lo, e_hi, tr)
    return e_lo, e_hi


def _patch_one_hot(oh_ref, slot, ridx_ref, c, e_lo, e_hi, tr):
    @pl.when(e_hi - e_lo > ONE_HOT_SLOTS)
    def _():
        oh_ref[slot] = _one_hot_rows_any(ridx_ref, c * ROW_CHUNK, e_lo, e_hi, tr)


def _dispatch_kernel(nrows_ref, used_ref, table_ref, goff_ref, next_goff_ref, x_ref, ridx_ref,
                     next_ridx_ref, xe_ref, stage, extra, oh, oh_extra, zeros, sem, xsem, zsem,
                     *, tr, capp):
    bi, ti = pl.program_id(0), pl.program_id(1)
    step = bi * pl.num_programs(1) + ti
    n_steps = pl.num_programs(0) * pl.num_programs(1)
    par = step % 2
    nch = (nrows_ref[bi, ti] + (ROW_CHUNK - 1)) // ROW_CHUNK
    x = x_ref[0]
    trash = pl.num_programs(0) * N_EXPERTS * capp
    n_big = DISPATCH_CHUNKS * PIECES

    def piece(src, j, dst, dsem):
        return pltpu.make_async_copy(
            src.at[pl.ds(j * ROW_ALIGN, ROW_ALIGN), :],
            xe_ref.at[pl.ds(pl.multiple_of(dst, ROW_ALIGN), ROW_ALIGN), :], dsem)

    def wait_big(parity):
        for j in range(n_big):
            piece(stage.at[parity], j, 0, sem.at[parity]).wait()

    def build_tile(rref, gref, parity):
        return [_build_one_hot(oh.at[parity], c, rref, gref, c, tr) for c in range(DISPATCH_CHUNKS)]

    def patch_tile(rref, parity, spans):
        for c, span in enumerate(spans):
            _patch_one_hot(oh.at[parity], c, rref, c, *span, tr)

    pl.when(step == 0)(lambda: patch_tile(ridx_ref, 0, build_tile(ridx_ref, goff_ref, 0)))
    pl.when(step >= 2)(lambda: wait_big(par))

    next_spans = build_tile(next_ridx_ref, next_goff_ref, 1 - par)
    rows = DISPATCH_CHUNKS * ROW_CHUNK
    stage[par] = _dot(oh[par].reshape(rows, oh.shape[3]), x).astype(BF16)
    for j in range(n_big):
        dst = table_ref[0, 0, j]
        dst = jnp.where(dst >= 0, dst, trash + (par * n_big + j) * ROW_ALIGN)
        piece(stage.at[par], j, dst, sem.at[par]).start()
    patch_tile(next_ridx_ref, 1 - par, next_spans)

    def one_more(c, carry):
        span = _build_one_hot(oh_extra, 0, ridx_ref, goff_ref, c, tr)
        _patch_one_hot(oh_extra, 0, ridx_ref, c, *span, tr)
        extra[...] = _dot(oh_extra[0], x).astype(BF16)
        for p in range(PIECES):
            dst = table_ref[0, 0, c * PIECES + p]
            dst = jnp.where(dst >= 0, dst, trash + (2 * n_big + p) * ROW_ALIGN)
            piece(extra, p, dst, xsem).start()
        for p in range(PIECES):
            piece(extra, p, 0, xsem).wait()
        return carry

    lax.fori_loop(DISPATCH_CHUNKS, nch, one_more, 0)

    @pl.when(step == n_steps - 1)
    def _():
        wait_big(par)
        pl.when(n_steps >= 2)(lambda: wait_big(1 - par))

    def zero_piece(row):
        return pltpu.make_async_copy(
            zeros, xe_ref.at[pl.ds(pl.multiple_of(row, ROW_ALIGN), ROW_ALIGN), :], zsem)

    def zero_rows(first, n_pieces):
        def start(k, carry):
            zero_piece(first + k * ROW_ALIGN).start()
            return carry

        def wait(k, carry):
            zero_piece(first + k * ROW_ALIGN).wait()
            return carry

        lax.fori_loop(0, n_pieces, start, 0)
        lax.fori_loop(0, n_pieces, wait, 0)

    @pl.when(ti == pl.num_programs(1) - 1)
    def _():
        zeros[...] = jnp.zeros(zeros.shape, BF16)
        for e in range(N_EXPERTS):
            used = used_ref[bi, e]
            zero_rows((bi * N_EXPERTS + e) * capp + used, (capp - used) // ROW_ALIGN)

        @pl.when(bi == pl.num_programs(0) - 1)
        def _():
            zero_rows(trash, TRASH_ROWS // ROW_ALIGN)


def _dispatch(h_ext, ridx5, table3, goff3, nrows, used, capp, tt):
    b, t, dx = h_ext.shape
    nt = t // tt
    tr = tt // LANES
    pl_lanes = table3.shape[2]
    nxt = lambda bi, ti: jnp.minimum(bi * nt + ti + 1, b * nt - 1)
    grid_spec = pltpu.PrefetchScalarGridSpec(
        num_scalar_prefetch=2,
        grid=(b, nt),
        in_specs=[pl.BlockSpec((1, 1, pl_lanes), lambda bi, ti, nr, us: (bi * nt + ti, 0, 0),
                               memory_space=pltpu.SMEM),
                  pl.BlockSpec((1, 1, LANES), lambda bi, ti, nr, us: (bi * nt + ti, 0, 0),
                               memory_space=pltpu.SMEM),
                  pl.BlockSpec((1, 1, LANES), lambda bi, ti, nr, us: (nxt(bi, ti), 0, 0),
                               memory_space=pltpu.SMEM),
                  pl.BlockSpec((1, tt, dx), lambda bi, ti, nr, us: (bi, ti, 0)),
                  pl.BlockSpec((1, N_EXPERTS, 1, tr, LANES), lambda bi, ti, nr, us: (bi, 0, ti, 0, 0)),
                  pl.BlockSpec((1, N_EXPERTS, 1, tr, LANES),
                               lambda bi, ti, nr, us: (nxt(bi, ti) // nt, 0, nxt(bi, ti) % nt, 0, 0))],
        out_specs=pl.BlockSpec(memory_space=pl.ANY),
        scratch_shapes=[pltpu.VMEM((2, DISPATCH_CHUNKS * ROW_CHUNK, dx), BF16),
                        pltpu.VMEM((ROW_CHUNK, dx), BF16),
                        pltpu.VMEM((2, DISPATCH_CHUNKS, ROW_CHUNK, tt), BF16),
                        pltpu.VMEM((1, ROW_CHUNK, tt), BF16), pltpu.VMEM((ROW_ALIGN, dx), BF16),
                        pltpu.SemaphoreType.DMA((2,)), pltpu.SemaphoreType.DMA(()),
                        pltpu.SemaphoreType.DMA(())],
    )
    return pl.pallas_call(
        functools.partial(_dispatch_kernel, tr=tr, capp=capp),
        grid_spec=grid_spec,
        out_shape=jax.ShapeDtypeStruct((b * N_EXPERTS * capp + TRASH_ROWS, dx), BF16),
        compiler_params=_cparams("arbitrary", "arbitrary"),
    )(nrows, used, table3, goff3, goff3, h_ext, ridx5, ridx5)


def _ffn_kernel(used_ref, x_ref, wg_ref, wu_ref, wd_ref, o_ref, *, rb):
    e, bi = pl.program_id(0), pl.program_id(1)
    capp, d = o_ref.shape[2], o_ref.shape[3]
    nblk = (used_ref[bi, e] + (rb - 1)) // rb
    lane = lax.broadcasted_iota(I32, (1, LANES), 1)
    own = (lane == e) | (lane == e + N_EXPERTS) | (lane == e + 2 * N_EXPERTS)

    def body(j, carry):
        rows = pl.ds(pl.multiple_of(j * rb, rb), rb)
        xb = x_ref[rows, :]
        xm = xb[:, :d]
        val = jnp.sum(jnp.where(own, xb[:, d:].astype(F32), 0.0), axis=1, keepdims=True)
        hid = (_silu(_dot(xm, wg_ref[0, 0])) * _dot(xm, wu_ref[0, 0])).astype(BF16)
        o_ref[0, 0, rows, :] = (_dot(hid, wd_ref[0, 0]) * val).astype(BF16)
        return carry

    def zero(j, carry):
        o_ref[0, 0, pl.ds(pl.multiple_of(j * rb, rb), rb), :] = jnp.zeros((rb, d), BF16)
        return carry

    lax.fori_loop(0, nblk, body, 0)
    lax.fori_loop(nblk, capp // rb, zero, 0)


def _ffn(xe, used, wg, wu, wd, layer, capp, rb):
    b, e_n = used.shape
    dx = xe.shape[1]
    d, f = wg.shape[2], wg.shape[3]
    grid_spec = pltpu.PrefetchScalarGridSpec(
        num_scalar_prefetch=1,
        grid=(e_n, b),
        in_specs=[pl.BlockSpec((capp, dx), lambda e, bi, u: (bi * e_n + e, 0)),
                  pl.BlockSpec((1, 1, d, f), lambda e, bi, u: (layer, e, 0, 0)),
                  pl.BlockSpec((1, 1, d, f), lambda e, bi, u: (layer, e, 0, 0)),
                  pl.BlockSpec((1, 1, f, d), lambda e, bi, u: (layer, e, 0, 0))],
        out_specs=pl.BlockSpec((1, 1, capp, d), lambda e, bi, u: (bi, e, 0, 0)),
    )
    return pl.pallas_call(
        functools.partial(_ffn_kernel, rb=rb),
        grid_spec=grid_spec,
        out_shape=jax.ShapeDtypeStruct((b, e_n, capp, d), BF16),
        compiler_params=_cparams("arbitrary", "arbitrary"),
    )(used, xe, wg, wu, wd)


def _combine_kernel(nrows_ref, table_ref, next_table_ref, goff_ref, next_goff_ref, ridx_ref,
                    next_ridx_ref, x_ref, g2_ref, fg_ref, ye_ref, o_ref,
                    stage, extra, oh, oh_extra, acc, sem, xsem, *, tr, final):
    bi, ti = pl.program_id(0), pl.program_id(1)
    step = bi * pl.num_programs(1) + ti
    n_steps = pl.num_programs(0) * pl.num_programs(1)
    par = step % 2

    def chunks(s):
        return (nrows_ref[s] + (ROW_CHUNK - 1)) // ROW_CHUNK

    nch = chunks(step)
    n_pre = jnp.minimum(nch, COMBINE_AHEAD)

    def piece(tbl, c, p, dst, dsem):
        src = tbl[0, 0, c * PIECES + p]
        return pltpu.make_async_copy(
            ye_ref.at[pl.ds(pl.multiple_of(jnp.maximum(src, 0), ROW_ALIGN), ROW_ALIGN), :],
            dst.at[pl.ds(p * ROW_ALIGN, ROW_ALIGN), :], dsem)

    def fetch_tile(tbl, parity, n):
        for c in range(COMBINE_AHEAD):
            @pl.when(c < n)
            def _(c=c):
                for p in range(PIECES):
                    piece(tbl, c, p, stage.at[parity, c], sem.at[parity, c]).start()

    def build_tile(rref, gref, parity):
        return [_build_one_hot(oh.at[parity], c, rref, gref, c, tr) for c in range(COMBINE_AHEAD)]

    def patch_tile(rref, parity, spans):
        for c, span in enumerate(spans):
            _patch_one_hot(oh.at[parity], c, rref, c, *span, tr)

    @pl.when(step == 0)
    def _():
        stage[...] = jnp.zeros(stage.shape, BF16)
        fetch_tile(table_ref, 0, n_pre)
        patch_tile(ridx_ref, 0, build_tile(ridx_ref, goff_ref, 0))

    @pl.when(step + 1 < n_steps)
    def _():
        fetch_tile(next_table_ref, 1 - par, jnp.minimum(chunks(step + 1), COMBINE_AHEAD))

    for c in range(COMBINE_AHEAD):
        @pl.when(c < n_pre)
        def _(c=c):
            for p in range(PIECES):
                piece(table_ref, c, p, stage.at[par, c], sem.at[par, c]).wait()

    next_spans = build_tile(next_ridx_ref, next_goff_ref, 1 - par)
    rows = COMBINE_AHEAD * ROW_CHUNK
    acc[...] = _dot_tn(oh[par].reshape(rows, oh.shape[3]), stage[par].reshape(rows, stage.shape[3]))
    patch_tile(next_ridx_ref, 1 - par, next_spans)

    def one_more(c, carry):
        for p in range(PIECES):
            piece(table_ref, c, p, extra, xsem).start()
        span = _build_one_hot(oh_extra, 0, ridx_ref, goff_ref, c, tr)
        _patch_one_hot(oh_extra, 0, ridx_ref, c, *span, tr)
        for p in range(PIECES):
            piece(table_ref, c, p, extra, xsem).wait()
        acc[...] += _dot_tn(oh_extra[0], extra[...])
        return carry

    lax.fori_loop(COMBINE_AHEAD, nch, one_more, 0)
    xo = x_ref[0] + g2_ref[0] * acc[...]
    if final:
        xo = xo * lax.rsqrt(jnp.mean(xo * xo, axis=-1, keepdims=True) + EPS) * fg_ref[...]
    o_ref[0] = xo


def _combine(ye_flat, ridx5, table3, goff3, nrows, x, g2, fg, tt, final):
    b, t, d = x.shape
    nt = t // tt
    tr = tt // LANES
    pl_lanes = table3.shape[2]
    nxt = lambda bi, ti: jnp.minimum(bi * nt + ti + 1, b * nt - 1)
    smem = lambda width, index: pl.BlockSpec((1, 1, width), index, memory_space=pltpu.SMEM)
    grid_spec = pltpu.PrefetchScalarGridSpec(
        num_scalar_prefetch=1,
        grid=(b, nt),
        in_specs=[smem(pl_lanes, lambda bi, ti, nr: (bi * nt + ti, 0, 0)),
                  smem(pl_lanes, lambda bi, ti, nr: (nxt(bi, ti), 0, 0)),
                  smem(LANES, lambda bi, ti, nr: (bi * nt + ti, 0, 0)),
                  smem(LANES, lambda bi, ti, nr: (nxt(bi, ti), 0, 0)),
                  pl.BlockSpec((1, N_EXPERTS, 1, tr, LANES), lambda bi, ti, nr: (bi, 0, ti, 0, 0)),
                  pl.BlockSpec((1, N_EXPERTS, 1, tr, LANES),
                               lambda bi, ti, nr: (nxt(bi, ti) // nt, 0, nxt(bi, ti) % nt, 0, 0)),
                  pl.BlockSpec((1, tt, d), lambda bi, ti, nr: (bi, ti, 0)),
                  pl.BlockSpec((1, 1, d), lambda bi, ti, nr: (bi, 0, 0)),
                  pl.BlockSpec((1, d), lambda bi, ti, nr: (0, 0)),
                  pl.BlockSpec(memory_space=pl.ANY)],
        out_specs=pl.BlockSpec((1, tt, d), lambda bi, ti, nr: (bi, ti, 0)),
        scratch_shapes=[pltpu.VMEM((2, COMBINE_AHEAD, ROW_CHUNK, d), BF16),
                        pltpu.VMEM((ROW_CHUNK, d), BF16),
                        pltpu.VMEM((2, COMBINE_AHEAD, ROW_CHUNK, tt), BF16),
                        pltpu.VMEM((1, ROW_CHUNK, tt), BF16), pltpu.VMEM((tt, d), F32),
                        pltpu.SemaphoreType.DMA((2, COMBINE_AHEAD)), pltpu.SemaphoreType.DMA(())],
    )
    return pl.pallas_call(
        functools.partial(_combine_kernel, tr=tr, final=final),
        grid_spec=grid_spec,
        out_shape=jax.ShapeDtypeStruct((b, t, d), F32),
        compiler_params=_cparams("arbitrary", "arbitrary"),
    )(nrows.reshape(b * nt), table3, table3, goff3, goff3, ridx5, ridx5, x, g2, fg, ye_flat)


def _moe_tiles(t):
    tt = min(512, t)
    t_sel = max(t, 8 * LANES)
    return tt, t_sel


def _ec_moe(x, g, shift, scale, gate, wr_pad, wg, wu, wd, layer, final_g, final):
    b, t, d = x.shape
    e_n = N_EXPERTS
    cap = EC_CAPACITY_FACTOR * t // e_n
    tt, t_sel = _moe_tiles(t)
    nt, tr = t // tt, tt // LANES
    capp = cap + ROW_ALIGN * nt
    rb = min(ROW_CHUNK, capp)
    capp = -(-capp // rb) * rb
    pl_lanes = -(-(tt + ROW_ALIGN) // LANES) * LANES

    h_ext, aff_t = _router(x, g, shift, scale, wr_pad, min(512, t))
    if t_sel > t:
        aff_t = jnp.pad(aff_t, ((0, 0), (0, 0), (0, t_sel - t)), constant_values=-1.0)
    aff_r = aff_t.reshape(b, e_n, t_sel // LANES, LANES)
    ridx, table, goff, used = _select(aff_r, cap, tr, capp, pl_lanes)
    ridx5 = ridx[:, :, :nt * tr].reshape(b, e_n, nt, tr, LANES)
    table3 = table[:, :nt * tr:tr].reshape(b * nt, 1, pl_lanes)
    goff3 = goff[:, :nt * tr:tr].reshape(b * nt, 1, LANES)
    nrows2 = goff[:, :nt * tr:tr, e_n]
    used2 = used[:, :, 0, 0]

    xe = _dispatch(h_ext, ridx5, table3, goff3, nrows2, used2, capp, tt)
    ye = _ffn(xe, used2, wg, wu, wd, layer, capp, rb)
    return _combine(ye.reshape(b * e_n * capp, d), ridx5, table3, goff3, nrows2, x, gate, final_g, tt,
                    final)


def kernel(x, c, ctx, c_ctx, ada_w, ada_b, norm_g, gla_w_in, gla_w_a2, gla_b_a2, gla_norm_g,
           gla_w_out, conv_w_in, conv_k, conv_w_out, router_w, expert_w_gate, expert_w_up,
           expert_w_down, final_norm_g):
    b, t, d = x.shape
    depth = ada_w.shape[0]
    assert depth == 2 and b + 1 <= 8
    dk = gla_w_a2.shape[3]
    dv = gla_w_out.shape[1]
    hk, hv = dk // GLA_HEADS, dv // GLA_HEADS

    cvec = jnp.zeros((8, d), F32).at[:b].set(c).at[b].set(c_ctx)
    mods = _ada(cvec, ada_w, ada_b)

    def mod_vectors(layer, rows):
        m = mods[layer, rows].reshape(rows.shape[0], 1, 6, d)
        return [m[:, :, i, :] for i in range(6)]

    rows_x = jnp.arange(b)
    rows_c = jnp.full((b,), b)
    wr_hi = router_w.astype(BF16)
    wr_lo = (router_w - wr_hi.astype(F32)).astype(BF16)
    wr_pad = [jnp.pad(jnp.concatenate([wr_hi[i], wr_lo[i]], axis=1), ((0, 0), (0, LANES - 2 * N_EXPERTS)))
              for i in range(depth)]
    fg = final_norm_g.reshape(1, d)

    sh1x, sc1x, g1x, sh2x, sc2x, g2x = mod_vectors(0, rows_x)
    sh1c, sc1c, g1c, sh2c, sc2c, g2c = mod_vectors(0, rows_c)
    n_main = 2 * dk + 2 * dv
    w_main = jnp.pad(gla_w_in[0], ((0, 0), (0, LANES - 2 * GLA_RANK))).astype(BF16)
    assert w_main.shape[1] == n_main + LANES
    w2 = jnp.zeros((2 * GLA_RANK, 2 * dk), F32)
    w2 = w2.at[:GLA_RANK, :dk].set(gla_w_a2[0, 0]).at[GLA_RANK:, dk:].set(gla_w_a2[0, 1])
    w2_hi = w2.astype(BF16)
    w2_lo = (w2 - w2_hi.astype(F32)).astype(BF16)
    w2 = jnp.concatenate([w2_hi, w2_hi, w2_lo, jnp.zeros((LANES - 6 * GLA_RANK, 2 * dk), BF16)])
    b2 = gla_b_a2[0].reshape(1, 2 * dk)
    gn = gla_norm_g[0].reshape(1, hv)
    w_out = gla_w_out[0].astype(BF16)
    g_n1 = norm_g[0, 0].reshape(1, d)
    g_n2 = norm_g[0, 1].reshape(1, d)

    tc = ctx.shape[1]
    qc, kc, vc, rc, lac = _gla_in(ctx, g_n1, sh1c, sc1c, w_main, w2, b2, min(512, tc))
    s0 = jnp.zeros((b, GLA_HEADS, hv, hk), F32)
    ocf, ocb, s_cf, s_cb = _gla_scan(qc, kc, vc, lac, s0, s0)
    qx, kx, vx, rx, lax_ = _gla_in(x, g_n1, sh1x, sc1x, w_main, w2, b2, 512)
    oxf, oxb, _, _, wg, wu, wd = _gla_scan(qx, kx, vx, lax_, s_cf, s_cb,
                                           cast=(expert_w_gate, expert_w_up, expert_w_down))
    wg, wu, wd = (w.reshape(s.shape) for w, s in
                  zip((wg, wu, wd), (expert_w_gate, expert_w_up, expert_w_down)))
    x = _gla_out(oxf, oxb, rx, x, gn, w_out, g1x, 512)
    x = _ec_moe(x, g_n2, sh2x, sc2x, g2x, wr_pad[0], wg, wu, wd, 0, fg, False)

    ctx = _gla_out(ocf, ocb, rc, ctx, gn, w_out, g1c, min(512, tc))
    ctx = _ec_moe(ctx, g_n2, sh2c, sc2c, g2c, wr_pad[0], wg, wu, wd, 0, fg, False)
    del ctx

    sh1x, sc1x, g1x, sh2x, sc2x, g2x = mod_vectors(1, rows_x)
    x = _conv_mix(x, norm_g[1, 0].reshape(1, d), sh1x, sc1x, conv_w_in[0].astype(BF16), conv_k[0],
                  conv_w_out[0].astype(BF16), g1x, GRID_W, 512)
    return _ec_moe(x, norm_g[1, 1].reshape(1, d), sh2x, sc2x, g2x, wr_pad[1], wg, wu, wd, 1, fg, True)
```
